```python
import math
import jax, jax.numpy as jnp
from jax import lax
import numpy as np

D_MODEL = 1024
BATCH = 16
SEQ = 2048
DEPTH = 2

MEM_LEN = 256
GDN_HEADS = 4
GDN_DK = 128
GDN_DV = 128
GDN_QK = GDN_HEADS * GDN_DK
GDN_V = GDN_HEADS * GDN_DV
GDN_CONV = 4
GDN_CHUNK = 64
FOX_HEADS = 8
FOX_DH = 64
FOX_W = FOX_HEADS * FOX_DH
FOX_BLOCK = 128
GLA_HEADS = 4
GLA_DK = 64
GLA_DV = 128
GLA_QK = GLA_HEADS * GLA_DK
GLA_V = GLA_HEADS * GLA_DV
GLA_RANK = 16
GLA_TAU = 16.0
GLA_CHUNK = 64
N_BRANCH = 3
XA_HEADS = 4
XA_DH = D_MODEL // XA_HEADS
MOE_GROUPS = 4
MOE_PER_GROUP = 8
MOE_EXPERTS = MOE_GROUPS * MOE_PER_GROUP
MOE_TOPK = 2
MOE_FF = D_MODEL // 4
MOE_BLOCK = 256
DEEPNORM_ALPHA = (2 * DEPTH) ** 0.25
DEEPNORM_BETA = (8 * DEPTH) ** -0.25
LN_EPS = 1e-5
RMS_EPS = 1e-6
IN_SPLITS = (2 * GDN_QK + GDN_V, GDN_HEADS, GDN_HEADS, GDN_V,
             3 * FOX_W, FOX_HEADS,
             GLA_QK, GLA_QK, GLA_V, GLA_V, GLA_RANK,
             N_BRANCH * D_MODEL)
N_IN = sum(IN_SPLITS)

kernel_name = 'hybrid_gdn_fox_gla_hmoe_block'

F32 = jnp.float32


def layer_norm(x, g, b):
    xf = x.astype(F32)
    mu = jnp.mean(xf, axis=-1, keepdims=True)
    var = jnp.mean(jnp.square(xf - mu), axis=-1, keepdims=True)
    return ((xf - mu) * lax.rsqrt(var + LN_EPS) * g.astype(F32) + b.astype(F32)).astype(x.dtype)


def rms_norm(x, w):
    xf = x.astype(F32)
    return (xf * lax.rsqrt(jnp.mean(xf * xf, axis=-1, keepdims=True) + RMS_EPS) * w.astype(F32)).astype(x.dtype)


def l2_normalize(x):
    xf = x.astype(F32)
    return (xf * lax.rsqrt(jnp.sum(xf * xf, axis=-1, keepdims=True) + RMS_EPS)).astype(x.dtype)


def causal_depthwise_conv(x, w):
    width, s = w.shape[0], x.shape[1]
    xp = jnp.pad(x, ((0, 0), (width - 1, 0), (0, 0)))
    return sum(xp[:, i:i + s, :] * w[i] for i in range(width))


def _to_chunks(t, n, c):
    b, s, h = t.shape[:3]
    t = t.astype(F32).reshape((b, n, c, h) + t.shape[3:])
    return jnp.moveaxis(t, 3, 1)


def gated_delta_rule_chunked(q, k, v, g, beta):
    out_dtype = v.dtype
    bsz, s, h, dk = q.shape
    dv = v.shape[-1]
    c = GDN_CHUNK
    n = s // c
    q = _to_chunks(q, n, c) * (dk ** -0.5)
    k = _to_chunks(k, n, c)
    v = _to_chunks(v, n, c)
    g = jnp.cumsum(_to_chunks(g, n, c), axis=-1)
    beta = _to_chunks(beta, n, c)
    causal = jnp.tril(jnp.ones((c, c), bool))
    strict = jnp.tril(jnp.ones((c, c), bool), -1)
    diff = g[..., :, None] - g[..., None, :]
    decay = jnp.where(causal, jnp.exp(jnp.where(causal, diff, 0.0)), 0.0)
    k_beta = k * beta[..., None]
    lower = jnp.where(strict, jnp.einsum('bhnid,bhnjd->bhnij', k_beta, k) * decay, 0.0)
    eye = jnp.eye(c, dtype=F32)
    t_inv = lax.linalg.triangular_solve(eye + lower, jnp.broadcast_to(eye, lower.shape),
                                        left_side=True, lower=True, unit_diagonal=True)
    u = jnp.einsum('bhnij,bhnje->bhnie', t_inv, v * beta[..., None])
    w = jnp.einsum('bhnij,bhnjd->bhnid', t_inv, k_beta * jnp.exp(g)[..., None])
    attn = jnp.where(causal, jnp.einsum('bhnid,bhnjd->bhnij', q, k) * decay, 0.0)
    q_dec = q * jnp.exp(g)[..., None]
    k_dec = k * jnp.exp(g[..., -1:] - g)[..., None]
    chunk_decay = jnp.exp(g[..., -1])

    def step(state, xs):
        attn_c, u_c, w_c, qd_c, kd_c, cd_c = xs
        v_new = u_c - jnp.einsum('bhcd,bhde->bhce', w_c, state)
        out = jnp.einsum('bhcd,bhde->bhce', qd_c, state) + jnp.einsum('bhij,bhje->bhie', attn_c, v_new)
        state = state * cd_c[..., None, None] + jnp.einsum('bhcd,bhce->bhde', kd_c, v_new)
        return state, out

    xs = tuple(jnp.moveaxis(t, 2, 0) for t in (attn, u, w, q_dec, k_dec, chunk_decay))
    _, out = lax.scan(step, jnp.zeros((bsz, h, dk, dv), F32), xs)
    return jnp.transpose(out, (1, 0, 3, 2, 4)).reshape(bsz, s, h, dv).astype(out_dtype)


def gla_chunked(q, k, v, log_a):
    out_dtype = v.dtype
    bsz, s, h, dk = q.shape
    dv = v.shape[-1]
    c = GLA_CHUNK
    n = s // c
    q = _to_chunks(q, n, c) * (dk ** -0.5)
    k = _to_chunks(k, n, c)
    v = _to_chunks(v, n, c)
    cum = jnp.cumsum(_to_chunks(log_a, n, c), axis=3)
    q_t = q * jnp.exp(cum)
    k_t = k * jnp.exp(-cum)
    causal = jnp.tril(jnp.ones((c, c), bool))
    attn = jnp.where(causal, jnp.einsum('bhnid,bhnjd->bhnij', q_t, k_t), 0.0)
    intra = jnp.einsum('bhnij,bhnje->bhnie', attn, v)
    k_dec = k * jnp.exp(cum[..., -1:, :] - cum)
    chunk_decay = jnp.exp(cum[..., -1, :])

    def step(state, xs):
        qt_c, kd_c, v_c, cd_c = xs
        out = jnp.einsum('bhcd,bhde->bhce', qt_c, state)
        state = state * cd_c[..., None] + jnp.einsum('bhcd,bhce->bhde', kd_c, v_c)
        return state, out

    xs = tuple(jnp.moveaxis(t, 2, 0) for t in (q_t, k_dec, v, chunk_decay))
    _, inter = lax.scan(step, jnp.zeros((bsz, h, dk, dv), F32), xs)
    out = intra + jnp.moveaxis(inter, 0, 2)
    return jnp.transpose(out, (0, 2, 3, 1, 4)).reshape(bsz, s, h, dv).astype(out_dtype)


def forgetting_attention(q, k, v, log_f):
    bsz, s, h, dh = q.shape
    nb = s // FOX_BLOCK
    cum_k = jnp.transpose(jnp.cumsum(log_f.astype(F32), axis=1), (0, 2, 1))
    kf = k.astype(F32)
    q_blocks = jnp.moveaxis(q.astype(F32).reshape(bsz, nb, FOX_BLOCK, h, dh), 1, 0)
    c_blocks = jnp.moveaxis(cum_k.reshape(bsz, h, nb, FOX_BLOCK), 2, 0)
    key_pos = jnp.arange(s)

    def block(args):
        q_blk, c_blk, start = args
        logits = (jnp.einsum('bqhd,bkhd->bhqk', q_blk, kf) * (dh ** -0.5)
                  + c_blk[..., :, None] - cum_k[:, :, None, :])
        query_pos = start + jnp.arange(FOX_BLOCK)
        logits = jnp.where(key_pos[None, :] <= query_pos[:, None], logits, -jnp.inf)
        p = jax.nn.softmax(logits, axis=-1)
        return jnp.einsum('bhqk,bkhd->bqhd', p.astype(v.dtype), v)

    out = lax.map(block, (q_blocks, c_blocks, jnp.arange(nb) * FOX_BLOCK))
    return jnp.moveaxis(out, 0, 1).reshape(bsz, s, h, dh)


def hybrid_token_mixer(x, w_in, gdn_conv_w, gdn_a_log, gdn_dt_bias, gdn_norm_w, fox_f_bias,
                       gla_w_gate2, gla_b_gate, gla_norm_w, p_gdn, p_fox, p_gla, b_merge, w_out):
    bsz, s, d = x.shape
    proj = x @ w_in
    offsets = np.cumsum(IN_SPLITS)[:-1].tolist()
    (gdn_qkv, gdn_b, gdn_a, gdn_z, fox_qkv, fox_f,
     gla_q, gla_k, gla_v, gla_r, gla_lr, merge_logits) = jnp.split(proj, offsets, axis=-1)

    gdn_qkv = jax.nn.silu(causal_depthwise_conv(gdn_qkv, gdn_conv_w))
    q_a, k_a, v_a = jnp.split(gdn_qkv, [GDN_QK, 2 * GDN_QK], axis=-1)
    q_a = l2_normalize(q_a.reshape(bsz, s, GDN_HEADS, GDN_DK))
    k_a = l2_normalize(k_a.reshape(bsz, s, GDN_HEADS, GDN_DK))
    v_a = v_a.reshape(bsz, s, GDN_HEADS, GDN_DV)
    beta_a = jax.nn.sigmoid(gdn_b)
    g_a = -jnp.exp(gdn_a_log) * jax.nn.softplus(gdn_a + gdn_dt_bias)
    o_a = gated_delta_rule_chunked(q_a, k_a, v_a, g_a, beta_a)
    o_a = rms_norm(o_a, gdn_norm_w) * jax.nn.silu(gdn_z.reshape(bsz, s, GDN_HEADS, GDN_DV))

    q_b, k_b, v_b = jnp.split(fox_qkv, 3, axis=-1)
    log_f = jax.nn.log_sigmoid((fox_f + fox_f_bias).astype(F32))
    o_b = forgetting_attention(q_b.reshape(bsz, s, FOX_HEADS, FOX_DH),
                               k_b.reshape(bsz, s, FOX_HEADS, FOX_DH),
                               v_b.reshape(bsz, s, FOX_HEADS, FOX_DH), log_f)

    log_a = jax.nn.log_sigmoid((gla_lr @ gla_w_gate2 + gla_b_gate).astype(F32)) / GLA_TAU
    o_c = gla_chunked(gla_q.reshape(bsz, s, GLA_HEADS, GLA_DK),
                      gla_k.reshape(bsz, s, GLA_HEADS, GLA_DK),
                      gla_v.reshape(bsz, s, GLA_HEADS, GLA_DV),
                      log_a.reshape(bsz, s, GLA_HEADS, GLA_DK))
    o_c = rms_norm(o_c, gla_norm_w) * jax.nn.silu(gla_r.reshape(bsz, s, GLA_HEADS, GLA_DV))

    gate = jax.nn.sigmoid(merge_logits + b_merge).reshape(bsz, s, N_BRANCH, d)
    merged = (gate[:, :, 0] * (o_a.reshape(bsz, s, GDN_V) @ p_gdn)
              + gate[:, :, 1] * (o_b.reshape(bsz, s, FOX_W) @ p_fox)
              + gate[:, :, 2] * (o_c.reshape(bsz, s, GLA_V) @ p_gla))
    return (merged @ w_out).astype(x.dtype)


def memory_cross_attention(x, mem, wq, wkv, wo):
    bsz, s, d = x.shape
    m = mem.shape[1]
    q = (x @ wq).reshape(bsz, s, XA_HEADS, XA_DH)
    k, v = jnp.split(mem @ wkv, 2, axis=-1)
    k = k.reshape(bsz, m, XA_HEADS, XA_DH)
    v = v.reshape(bsz, m, XA_HEADS, XA_DH)
    logits = jnp.einsum('bqhd,bkhd->bhqk', q.astype(F32), k.astype(F32)) * (XA_DH ** -0.5)
    p = jax.nn.softmax(logits, axis=-1).astype(v.dtype)
    o = jnp.einsum('bhqk,bkhd->bqhd', p, v).reshape(bsz, s, d)
    return (o @ wo).astype(x.dtype)


def routed_experts(xf, expert_idx, expert_w, w_gate, w_up, w_down):
    n, d = xf.shape
    kk = expert_idx.shape[1]
    e = w_gate.shape[0]
    nk = n * kk
    flat_e = expert_idx.reshape(nk)
    order = jnp.argsort(flat_e)
    sorted_e = flat_e[order]
    sorted_tok = order // kk
    counts = jnp.zeros((e,), jnp.int32).at[flat_e].add(1)
    padded = (counts + MOE_BLOCK - 1) // MOE_BLOCK * MOE_BLOCK
    pad_end = jnp.cumsum(padded)
    pad_start = pad_end - padded
    start = jnp.cumsum(counts) - counts
    dest = pad_start[sorted_e] + (jnp.arange(nk, dtype=jnp.int32) - start[sorted_e])
    total = (nk + MOE_BLOCK - 1) // MOE_BLOCK * MOE_BLOCK + e * MOE_BLOCK
    n_blocks = total // MOE_BLOCK
    slot_tok = jnp.full((total,), n, jnp.int32).at[dest].set(sorted_tok)
    blk_expert = jnp.minimum(
        jnp.searchsorted(pad_end, jnp.arange(n_blocks, dtype=jnp.int32) * MOE_BLOCK, side='right'), e - 1)
    x_pad = jnp.concatenate([xf, jnp.zeros((1, d), xf.dtype)], axis=0)
    xb = x_pad[slot_tok].reshape(n_blocks, MOE_BLOCK, d)

    def one_block(args):
        x_blk, ex = args
        hidden = jax.nn.silu(x_blk @ w_gate[ex]) * (x_blk @ w_up[ex])
        return hidden @ w_down[ex]

    yb = lax.map(one_block, (xb, blk_expert)).reshape(total, d)
    slot_of_assign = jnp.zeros((nk,), jnp.int32).at[order].set(dest)
    y = yb[slot_of_assign].reshape(n, kk, d)
    return jnp.einsum('nk,nkd->nd', expert_w, y)


def hierarchical_moe(x, w_group, b_group, w_expert, b_expert, w_gate, w_up, w_down):
    bsz, s, d = x.shape
    xf = x.reshape(bsz * s, d)
    p_group = jax.nn.softmax((xf @ w_group + b_group).astype(F32), axis=-1)
    p_top, g_sel = lax.top_k(p_group, 1)
    e_logits = (xf @ w_expert + b_expert).astype(F32).reshape(-1, MOE_GROUPS, MOE_PER_GROUP)
    sel_logits = jnp.take_along_axis(e_logits, g_sel[:, :, None], axis=1)[:, 0]
    p_in = jax.nn.softmax(sel_logits, axis=-1)
    w_top, e_local = lax.top_k(p_in, MOE_TOPK)
    w_top = w_top / jnp.sum(w_top, axis=-1, keepdims=True) * p_top
    e_global = g_sel * MOE_PER_GROUP + e_local
    y = routed_experts(xf, e_global, w_top.astype(x.dtype), w_gate, w_up, w_down)
    return y.reshape(bsz, s, d).astype(x.dtype)


def setup_inputs(seed: int = 0) -> dict:
    key = jax.random.key(seed)
    ks = iter(jax.random.split(key, 40))
    L, D = DEPTH, D_MODEL

    def normal(shape, scale):
        return jax.random.normal(next(ks), shape, F32) * scale

    def uniform(shape, lo, hi):
        return jax.random.uniform(next(ks), shape, F32, minval=lo, maxval=hi)

    dt = jnp.exp(uniform((L, GDN_HEADS), math.log(1e-3), math.log(0.1)))
    return {
        'x': normal((BATCH, SEQ, D), 1.0),
        'mem': normal((BATCH, MEM_LEN, D), 1.0),
        'w_in': normal((L, D, N_IN), D ** -0.5),
        'gdn_conv_w': normal((L, GDN_CONV, 2 * GDN_QK + GDN_V), GDN_CONV ** -0.5),
        'gdn_a_log': jnp.log(uniform((L, GDN_HEADS), 1.0, 16.0)),
        'gdn_dt_bias': dt + jnp.log(-jnp.expm1(-dt)),
        'gdn_norm_w': 1.0 + normal((L, GDN_DV), 0.02),
        'fox_f_bias': uniform((L, FOX_HEADS), 2.0, 5.0),
        'gla_w_gate2': normal((L, GLA_RANK, GLA_QK), GLA_RANK ** -0.5),
        'gla_b_gate': normal((L, GLA_QK), 0.1),
        'gla_norm_w': 1.0 + normal((L, GLA_DV), 0.02),
        'p_gdn': normal((L, GDN_V, D), GDN_V ** -0.5),
        'p_fox': normal((L, FOX_W, D), FOX_W ** -0.5),
        'p_gla': normal((L, GLA_V, D), GLA_V ** -0.5),
        'b_merge': normal((L, N_BRANCH * D), 0.1),
        'w_out': normal((L, D, D), DEEPNORM_BETA * D ** -0.5),
        'ln1_g': 1.0 + normal((L, D), 0.02),
        'ln1_b': normal((L, D), 0.02),
        'xa_wq': normal((L, D, D), D ** -0.5),
        'xa_wkv': normal((L, D, 2 * D), D ** -0.5),
        'xa_wo': normal((L, D, D), DEEPNORM_BETA * D ** -0.5),
        'ln2_g': 1.0 + normal((L, D), 0.02),
        'ln2_b': normal((L, D), 0.02),
        'moe_w_group': normal((L, D, MOE_GROUPS), D ** -0.5),
        'moe_b_group': normal((L, MOE_GROUPS), 0.01),
        'moe_w_expert': normal((L, D, MOE_EXPERTS), D ** -0.5),
        'moe_b_expert': normal((L, MOE_EXPERTS), 0.01),
        'moe_w_gate': normal((L, MOE_EXPERTS, D, MOE_FF), D ** -0.5),
        'moe_w_up': normal((L, MOE_EXPERTS, D, MOE_FF), D ** -0.5),
        'moe_w_down': normal((L, MOE_EXPERTS, MOE_FF, D), DEEPNORM_BETA * MOE_FF ** -0.5),
        'ln3_g': 1.0 + normal((L, D), 0.02),
        'ln3_b': normal((L, D), 0.02),
    }


def reference(x, mem, w_in, gdn_conv_w, gdn_a_log, gdn_dt_bias, gdn_norm_w, fox_f_bias,
              gla_w_gate2, gla_b_gate, gla_norm_w, p_gdn, p_fox, p_gla, b_merge, w_out,
              ln1_g, ln1_b, xa_wq, xa_wkv, xa_wo, ln2_g, ln2_b,
              moe_w_group, moe_b_group, moe_w_expert, moe_b_expert, moe_w_gate, moe_w_up, moe_w_down,
              ln3_g, ln3_b):
    for l in range(DEPTH):
        mix = hybrid_token_mixer(x, w_in[l], gdn_conv_w[l], gdn_a_log[l], gdn_dt_bias[l], gdn_norm_w[l],
                                 fox_f_bias[l], gla_w_gate2[l], gla_b_gate[l], gla_norm_w[l],
                                 p_gdn[l], p_fox[l], p_gla[l], b_merge[l], w_out[l])
        x = layer_norm(DEEPNORM_ALPHA * x + mix, ln1_g[l], ln1_b[l])
        xa = memory_cross_attention(x, mem, xa_wq[l], xa_wkv[l], xa_wo[l])
        x = layer_norm(DEEPNORM_ALPHA * x + xa, ln2_g[l], ln2_b[l])
        ff = hierarchical_moe(x, moe_w_group[l], moe_b_group[l], moe_w_expert[l], moe_b_expert[l],
                              moe_w_gate[l], moe_w_up[l], moe_w_down[l])
        x = layer_norm(DEEPNORM_ALPHA * x + ff, ln3_g[l], ln3_b[l])
    return x
```

```python
import functools

import jax
import jax.numpy as jnp
from jax import lax
from jax.experimental import pallas as pl
from jax.experimental.pallas import tpu as pltpu

F32 = jnp.float32
BF16 = jnp.bfloat16
I32 = jnp.int32
HIGHEST = lax.Precision.HIGHEST

D_MODEL = 1024
DEPTH = 2
GDN_HEADS = 4
GDN_DK = 128
GDN_DV = 128
GDN_QK = GDN_HEADS * GDN_DK
GDN_V = GDN_HEADS * GDN_DV
GDN_CONV = 4
CHUNK = 64
FOX_HEADS = 8
FOX_DH = 64
FOX_W = FOX_HEADS * FOX_DH
GLA_HEADS = 4
GLA_DK = 64
GLA_DV = 128
GLA_QK = GLA_HEADS * GLA_DK
GLA_V = GLA_HEADS * GLA_DV
GLA_RANK = 16
GLA_TAU = 16.0
N_BRANCH = 3
XA_HEADS = 4
XA_DH = D_MODEL // XA_HEADS
MOE_GROUPS = 4
MOE_PER_GROUP = 8
MOE_EXPERTS = MOE_GROUPS * MOE_PER_GROUP
MOE_FF = D_MODEL // 4
MOE_BLOCK = 256
DEEPNORM_ALPHA = (2 * DEPTH) ** 0.25
LN_EPS = 1e-5
RMS_EPS = 1e-6

LANES = 128
CONV_PAD = 16
NEG_BIG = -1e30
VMEM_LIMIT = 56 * 1024 * 1024

C_GDN_QKV = 0
C_GDN_Z = 1536
C_FOX_Q = 2048
C_FOX_K = 2560
C_FOX_V = 3072
C_GLA_Q = 3584
C_GLA_K = 3840
C_GLA_V = 4096
C_GLA_R = 4608
C_MERGE = 5120
N_BIG = 8192
G_B, G_A, G_F, G_LR = 0, 4, 8, 16


def _cparams(*sem):
    return pltpu.CompilerParams(dimension_semantics=sem, vmem_limit_bytes=VMEM_LIMIT)


def _sigmoid(x):
    return 1.0 / (1.0 + jnp.exp(-x))


def _silu(x):
    return x * _sigmoid(x)


def _softplus(x):
    return jnp.maximum(x, 0.0) + jnp.log(1.0 + jnp.exp(-jnp.abs(x)))


def _log_sigmoid(x):
    return jnp.minimum(x, 0.0) - jnp.log(1.0 + jnp.exp(-jnp.abs(x)))


def _layer_norm(y, g, b):
    mu = jnp.mean(y, axis=-1, keepdims=True)
    yc = y - mu
    var = jnp.mean(yc * yc, axis=-1, keepdims=True)
    return yc * lax.rsqrt(var + LN_EPS) * g + b


def _dot(a, b):
    return jnp.dot(a.astype(BF16), b.astype(BF16), preferred_element_type=F32)


def _dot_nt(a, b):
    return lax.dot_general(a.astype(BF16), b.astype(BF16), (((1,), (1,)), ((), ())),
                           preferred_element_type=F32)


def _dot_tn(a, b):
    return lax.dot_general(a.astype(BF16), b.astype(BF16), (((0,), (0,)), ((), ())),
                           preferred_element_type=F32)


def _dot_hi(a, b):
    return jnp.dot(a, b, precision=HIGHEST, preferred_element_type=F32)


def _dot_tn_hi(a, b):
    return lax.dot_general(a, b, (((0,), (0,)), ((), ())), precision=HIGHEST, preferred_element_type=F32)


def _tri(n, kind):
    r = lax.broadcasted_iota(I32, (n, n), 0)
    c = lax.broadcasted_iota(I32, (n, n), 1)
    if kind == "lower":
        return r >= c
    if kind == "strict":
        return r > c
    return r <= c


def _mm_kernel(a_ref, b_ref, o_ref):
    o_ref[...] = jnp.dot(a_ref[...].astype(BF16), b_ref[...], preferred_element_type=F32).astype(o_ref.dtype)


def _mm_hi_kernel(a_ref, b_ref, o_ref):
    o_ref[...] = _dot_hi(a_ref[...].astype(F32), b_ref[...]).astype(o_ref.dtype)


def _matmul(a, b, out_dtype, tm, tn, hi=False):
    m, k = a.shape
    n = b.shape[1]
    tm, tn = min(tm, m), min(tn, n)
    return pl.pallas_call(
        _mm_hi_kernel if hi else _mm_kernel,
        grid=(m // tm, n // tn),
        in_specs=[pl.BlockSpec((tm, k), lambda i, j: (i, 0)), pl.BlockSpec((k, tn), lambda i, j: (0, j))],
        out_specs=pl.BlockSpec((tm, tn), lambda i, j: (i, j)),
        out_shape=jax.ShapeDtypeStruct((m, n), out_dtype),
        compiler_params=_cparams("parallel", "parallel"),
        name="proj_hi" if hi else "proj",
    )(a, b)


def _fcum_kernel(g_ref, fb_ref, col_ref, row_ref, *, seq, blk):
    tril = _tri(blk, "lower").astype(F32)
    triu = _tri(blk, "upper").astype(F32)
    carry_row = jnp.zeros((1, LANES), F32)
    carry_col = jnp.zeros((LANES, 1), F32)
    for i in range(seq // blk):
        lf = _log_sigmoid(g_ref[i * blk:(i + 1) * blk, :] + fb_ref[...])
        cum = _dot_hi(tril, lf) + carry_row
        cum_t = _dot_tn_hi(lf, triu) + carry_col
        col_ref[i * blk:(i + 1) * blk, :] = cum
        row_ref[0, :, i * blk:(i + 1) * blk] = cum_t[G_F:G_F + FOX_HEADS, :]
        carry_row = cum[blk - 1:blk, :]
        carry_col = cum_t[:, blk - 1:blk]


def _fox_cum(gates, fb_vec, bsz, seq):
    blk = min(256, seq)
    return pl.pallas_call(
        functools.partial(_fcum_kernel, seq=seq, blk=blk),
        grid=(bsz,),
        in_specs=[pl.BlockSpec((seq, LANES), lambda b: (b, 0)), pl.BlockSpec((1, LANES), lambda b: (0, 0))],
        out_specs=[pl.BlockSpec((seq, LANES), lambda b: (b, 0)),
                   pl.BlockSpec((1, FOX_HEADS, seq), lambda b: (b, 0, 0))],
        out_shape=[jax.ShapeDtypeStruct((bsz * seq, LANES), F32),
                   jax.ShapeDtypeStruct((bsz, FOX_HEADS, seq), F32)],
        compiler_params=_cparams("parallel"),
        name="fox_cum",
    )(gates, fb_vec)


def _fox_kernel(q_ref, k_ref, v_ref, cc_ref, cr_ref, o_ref, *, tq):
    p = pl.program_id(1)
    qi = pl.program_id(2)
    q = q_ref[...]
    lane = lax.broadcasted_iota(I32, (tq, LANES), 1)
    cc = cc_ref[...]
    qpos = qi * tq + lax.broadcasted_iota(I32, (tq, tq), 0)
    kofs = lax.broadcasted_iota(I32, (tq, tq), 1)
    scale = FOX_DH ** -0.5

    def one_head(hh):
        head_lanes = (lane >= FOX_DH * hh) & (lane < FOX_DH * (hh + 1))
        qh = jnp.where(head_lanes, q, jnp.zeros_like(q)) * jnp.asarray(scale, BF16)
        hidx = 2 * p + hh
        c_q = jnp.sum(jnp.where(lane == G_F + hidx, cc, 0.0), axis=1, keepdims=True)

        def body(j, carry):
            m, l, acc = carry
            start = pl.multiple_of(j * tq, tq)
            kj = k_ref[pl.ds(start, tq), :]
            vj = v_ref[pl.ds(start, tq), :]
            c_k = cr_ref[0, pl.ds(hidx, 1), pl.ds(start, tq)]
            s = _dot_nt(qh, kj) + c_q - c_k
            s = jnp.where(start + kofs <= qpos, s, NEG_BIG)
            m_new = jnp.maximum(m, jnp.max(s, axis=1, keepdims=True))
            a = jnp.exp(m - m_new)
            pr = jnp.exp(s - m_new)
            l = a * l + jnp.sum(pr, axis=1, keepdims=True)
            acc = a * acc + _dot(pr, vj)
            return m_new, l, acc

        init = (jnp.full((tq, 1), NEG_BIG, F32), jnp.zeros((tq, 1), F32), jnp.zeros((tq, LANES), F32))
        _, l, acc = lax.fori_loop(0, qi + 1, body, init)
        return acc / l

    o0 = one_head(0)
    o1 = one_head(1)
    o_ref[...] = jnp.where(lane < FOX_DH, o0, o1).astype(o_ref.dtype)


def _fox_attention(proj_big, cum_col, cum_row, bsz, seq):
    tq = min(256, seq)
    nq = seq // tq
    n = bsz * seq
    qb, kb, vb = C_FOX_Q // LANES, C_FOX_K // LANES, C_FOX_V // LANES
    return pl.pallas_call(
        functools.partial(_fox_kernel, tq=tq),
        grid=(bsz, FOX_HEADS // 2, nq),
        in_specs=[
            pl.BlockSpec((tq, LANES), lambda b, p, i: (b * nq + i, qb + p)),
            pl.BlockSpec((seq, LANES), lambda b, p, i: (b, kb + p)),
            pl.BlockSpec((seq, LANES), lambda b, p, i: (b, vb + p)),
            pl.BlockSpec((tq, LANES), lambda b, p, i: (b * nq + i, 0)),
            pl.BlockSpec((1, FOX_HEADS, seq), lambda b, p, i: (b, 0, 0)),
        ],
        out_specs=pl.BlockSpec((tq, LANES), lambda b, p, i: (b * nq + i, p)),
        out_shape=jax.ShapeDtypeStruct((n, FOX_W), BF16),
        compiler_params=_cparams("parallel", "parallel", "parallel"),
        name="fox_attn",
    )(proj_big, proj_big, proj_big, cum_col, cum_row)


def _unit_lower_inverse(low):
    c = low.shape[0]
    eye = (lax.broadcasted_iota(I32, (c, c), 0) == lax.broadcasted_iota(I32, (c, c), 1)).astype(F32)
    inv = eye - low
    power = _dot_hi(low, low)
    span = 2
    while span < c:
        inv = inv + _dot_hi(inv, power)
        span *= 2
        if span < c:
            power = _dot_hi(power, power)
    return inv


def _gdn_kernel(qkv_ref, z_ref, g_ref, cw_ref, nega_ref, dt_ref, nw_ref, o_ref, pad_ref, st_ref, *, seq):
    c = CHUNK
    pad_ref[0:CONV_PAD, :] = jnp.zeros((CONV_PAD, 3 * GDN_QK), BF16)
    pad_ref[CONV_PAD:CONV_PAD + seq, :] = qkv_ref[...]
    st_ref[...] = jnp.zeros_like(st_ref)
    causal = _tri(c, "lower")
    strict = _tri(c, "strict")
    tril = causal.astype(F32)
    triu = _tri(c, "upper").astype(F32)
    cw = cw_ref[...]
    neg_a = -jnp.exp(nega_ref[...])
    lane1 = lax.broadcasted_iota(I32, (1, LANES), 1)
    neg_a = jnp.where((lane1 >= G_A) & (lane1 < G_A + GDN_HEADS), neg_a, 0.0)
    dt = dt_ref[...]
    nw = nw_ref[...]

    def chunk_body(ci, _):
        r0 = pl.multiple_of(ci * c, c)
        win = pad_ref[pl.ds(r0, c + CONV_PAD), :]
        gs = g_ref[pl.ds(r0, c), :]
        beta_all = _sigmoid(gs)
        g_all = neg_a * _softplus(gs + dt)
        cum = _dot_hi(tril, g_all)
        cum_t = _dot_tn_hi(g_all, triu)
        z = z_ref[pl.ds(r0, c), :].astype(F32)

        def conv(col0):
            xw = win[:, col0:col0 + LANES].astype(F32)
            acc = xw[CONV_PAD - 3:CONV_PAD - 3 + c] * cw[0:1, col0:col0 + LANES]
            for i in range(1, GDN_CONV):
                acc = acc + xw[CONV_PAD - 3 + i:CONV_PAD - 3 + i + c] * cw[i:i + 1, col0:col0 + LANES]
            return _silu(acc)

        def l2n(x):
            return x * lax.rsqrt(jnp.sum(x * x, axis=-1, keepdims=True) + RMS_EPS)

        outs = []
        for h in range(GDN_HEADS):
            q = l2n(conv(h * GDN_DK)) * (GDN_DK ** -0.5)
            k = l2n(conv(GDN_QK + h * GDN_DK))
            v = conv(2 * GDN_QK + h * GDN_DV)
            beta = beta_all[:, G_B + h:G_B + h + 1]
            gc = cum[:, G_A + h:G_A + h + 1]
            gr = cum_t[G_A + h:G_A + h + 1, :]
            glast = gc[c - 1:c, :]
            eg = jnp.exp(gc)
            decay = jnp.where(causal, jnp.exp(jnp.where(causal, gc - gr, 0.0)), 0.0)
            kb = k * beta
            low = jnp.where(strict, _dot_nt(kb, k) * decay, 0.0)
            t_inv = _unit_lower_inverse(low)
            u = _dot(t_inv, v * beta)
            w = _dot(t_inv, kb * eg)
            attn = jnp.where(causal, _dot_nt(q, k) * decay, 0.0)
            qd = q * eg
            kd = k * jnp.exp(glast - gc)
            state = st_ref[h]
            v_new = u - _dot(w, state)
            out = _dot(qd, state) + _dot(attn, v_new)
            st_ref[h] = state * jnp.exp(glast) + _dot_tn(kd, v_new)
            rms = out * lax.rsqrt(jnp.mean(out * out, axis=-1, keepdims=True) + RMS_EPS) * nw
            outs.append(rms * _silu(z[:, h * GDN_DV:(h + 1) * GDN_DV]))
        o_ref[pl.ds(r0, c), :] = jnp.concatenate(outs, axis=1).astype(o_ref.dtype)
        return 0

    lax.fori_loop(0, seq // c, chunk_body, 0)


def _gdn(proj_big, gates, conv_w, a_vec, dt_vec, norm_w, bsz, seq):
    n = bsz * seq
    return pl.pallas_call(
        functools.partial(_gdn_kernel, seq=seq),
        grid=(bsz,),
        in_specs=[
            pl.BlockSpec((seq, 3 * GDN_QK), lambda b: (b, 0)),
            pl.BlockSpec((seq, GDN_V), lambda b: (b, C_GDN_Z // GDN_V)),
            pl.BlockSpec((seq, LANES), lambda b: (b, 0)),
            pl.BlockSpec((GDN_CONV, 3 * GDN_QK), lambda b: (0, 0)),
            pl.BlockSpec((1, LANES), lambda b: (0, 0)),
            pl.BlockSpec((1, LANES), lambda b: (0, 0)),
            pl.BlockSpec((1, GDN_DV), lambda b: (0, 0)),
        ],
        out_specs=pl.BlockSpec((seq, GDN_V), lambda b: (b, 0)),
        out_shape=jax.ShapeDtypeStruct((n, GDN_V), BF16),
        scratch_shapes=[pltpu.VMEM((seq + CONV_PAD, 3 * GDN_QK), BF16),
                        pltpu.VMEM((GDN_HEADS, GDN_DK, GDN_DV), F32)],
        compiler_params=_cparams("parallel"),
        name="gdn",
    )(proj_big, proj_big, gates, conv_w, a_vec, dt_vec, norm_w)


def _gla_kernel(q_ref, k_ref, v_ref, r_ref, g_ref, w2_ref, bg_ref, nw_ref, o_ref, st_ref, *, seq):
    c = CHUNK
    st_ref[...] = jnp.zeros_like(st_ref)
    causal = _tri(c, "lower")
    tril = causal.astype(F32)
    w2 = w2_ref[...]
    bg = bg_ref[...]
    nw = nw_ref[...]
    lane1 = lax.broadcasted_iota(I32, (1, LANES), 1)

    def chunk_body(ci, _):
        r0 = pl.multiple_of(ci * c, c)
        gs = g_ref[pl.ds(r0, c), :]
        log_a = _log_sigmoid(_dot_hi(gs, w2) + bg) * (1.0 / GLA_TAU)
        cum = _dot_hi(tril, log_a)
        clast = cum[c - 1:c, :]
        q = q_ref[pl.ds(r0, c), :].astype(F32)
        k = k_ref[pl.ds(r0, c), :].astype(F32)
        v = v_ref[pl.ds(r0, c), :]
        r = r_ref[pl.ds(r0, c), :].astype(F32)
        q_t = q * jnp.exp(cum) * (GLA_DK ** -0.5)
        k_t = k * jnp.exp(-cum)
        k_dec = k * jnp.exp(clast - cum)
        cdec = jnp.exp(clast)
        outs = []
        for h in range(GLA_HEADS):
            p, hh = divmod(h, 2)
            sl = slice(p * LANES, (p + 1) * LANES)
            head_lanes = (lane1 >= GLA_DK * hh) & (lane1 < GLA_DK * (hh + 1))
            qh = jnp.where(head_lanes, q_t[:, sl], 0.0)
            attn = jnp.where(causal, _dot_nt(qh, k_t[:, sl]), 0.0)
            vh = v[:, h * GLA_DV:(h + 1) * GLA_DV]
            state_t = st_ref[h]
            out = _dot(attn, vh) + _dot_nt(qh, state_t)
            st_ref[h] = state_t * cdec[:, sl] + _dot_tn(vh, jnp.where(head_lanes, k_dec[:, sl], 0.0))
            rms = out * lax.rsqrt(jnp.mean(out * out, axis=-1, keepdims=True) + RMS_EPS) * nw
            outs.append(rms * _silu(r[:, h * GLA_DV:(h + 1) * GLA_DV]))
        o_ref[pl.ds(r0, c), :] = jnp.concatenate(outs, axis=1).astype(o_ref.dtype)
        return 0

    lax.fori_loop(0, seq // c, chunk_body, 0)


def _gla(proj_big, gates, w2_pad, b_gate, norm_w, bsz, seq):
    n = bsz * seq
    return pl.pallas_call(
        functools.partial(_gla_kernel, seq=seq),
        grid=(bsz,),
        in_specs=[
            pl.BlockSpec((seq, GLA_QK), lambda b: (b, C_GLA_Q // GLA_QK)),
            pl.BlockSpec((seq, GLA_QK), lambda b: (b, C_GLA_K // GLA_QK)),
            pl.BlockSpec((seq, GLA_V), lambda b: (b, C_GLA_V // GLA_V)),
            pl.BlockSpec((seq, GLA_V), lambda b: (b, C_GLA_R // GLA_V)),
            pl.BlockSpec((seq, LANES), lambda b: (b, 0)),
            pl.BlockSpec((LANES, GLA_QK), lambda b: (0, 0)),
            pl.BlockSpec((1, GLA_QK), lambda b: (0, 0)),
            pl.BlockSpec((1, GLA_DV), lambda b: (0, 0)),
        ],
        out_specs=pl.BlockSpec((seq, GLA_V), lambda b: (b, 0)),
        out_shape=jax.ShapeDtypeStruct((n, GLA_V), BF16),
        scratch_shapes=[pltpu.VMEM((GLA_HEADS, GLA_DV, LANES), F32)],
        compiler_params=_cparams("parallel"),
        name="gla",
    )(proj_big, proj_big, proj_big, proj_big, gates, w2_pad, b_gate, norm_w)


def _merge_kernel(oa_ref, ob_ref, oc_ref, m0_ref, m1_ref, m2_ref, x_ref, pa_ref, pb_ref, pc_ref, bm_ref,
                  wo_ref, g_ref, b_ref, o_ref):
    bm = bm_ref[...]
    merged = _sigmoid(m0_ref[...].astype(F32) + bm[0:1]) * jnp.dot(oa_ref[...], pa_ref[...],
                                                                   preferred_element_type=F32)
    merged += _sigmoid(m1_ref[...].astype(F32) + bm[1:2]) * jnp.dot(ob_ref[...], pb_ref[...],
                                                                    preferred_element_type=F32)
    merged += _sigmoid(m2_ref[...].astype(F32) + bm[2:3]) * jnp.dot(oc_ref[...], pc_ref[...],
                                                                    preferred_element_type=F32)
    mix = jnp.dot(merged.astype(BF16), wo_ref[...], preferred_element_type=F32)
    o_ref[...] = _layer_norm(DEEPNORM_ALPHA * x_ref[...] + mix, g_ref[...], b_ref[...])


def _merge(o_a, o_b, o_c, proj_big, x, p_a, p_b, p_c, b_merge, w_out, ln_g, ln_b):
    n = x.shape[0]
    tm = min(512, n)
    mb = C_MERGE // D_MODEL
    row = lambda i: (i, 0)
    full = lambda i: (0, 0)
    return pl.pallas_call(
        _merge_kernel,
        grid=(n // tm,),
        in_specs=[
            pl.BlockSpec((tm, GDN_V), row), pl.BlockSpec((tm, FOX_W), row), pl.BlockSpec((tm, GLA_V), row),
            pl.BlockSpec((tm, D_MODEL), lambda i: (i, mb)),
            pl.BlockSpec((tm, D_MODEL), lambda i: (i, mb + 1)),
            pl.BlockSpec((tm, D_MODEL), lambda i: (i, mb + 2)),
            pl.BlockSpec((tm, D_MODEL), row),
            pl.BlockSpec((GDN_V, D_MODEL), full), pl.BlockSpec((FOX_W, D_MODEL), full),
            pl.BlockSpec((GLA_V, D_MODEL), full),
            pl.BlockSpec((N_BRANCH, D_MODEL), full),
            pl.BlockSpec((D_MODEL, D_MODEL), full),
            pl.BlockSpec((1, D_MODEL), full), pl.BlockSpec((1, D_MODEL), full),
        ],
        out_specs=pl.BlockSpec((tm, D_MODEL), row),
        out_shape=jax.ShapeDtypeStruct((n, D_MODEL), F32),
        compiler_params=_cparams("parallel"),
        name="merge_out",
    )(o_a, o_b, o_c, proj_big, proj_big, proj_big, x, p_a, p_b, p_c, b_merge, w_out, ln_g, ln_b)


def _xattn_kernel(x_ref, kv_ref, wq_ref, wo_ref, g_ref, b_ref, o_ref):
    x = x_ref[...]
    q = jnp.dot(x.astype(BF16), wq_ref[...], preferred_element_type=F32).astype(BF16)
    kv = kv_ref[0]
    xa = jnp.zeros(x.shape, F32)
    for h in range(XA_HEADS):
        sl = slice(h * XA_DH, (h + 1) * XA_DH)
        s = _dot_nt(q[:, sl], kv[:, sl]) * (XA_DH ** -0.5)
        s = s - jnp.max(s, axis=-1, keepdims=True)
        e = jnp.exp(s)
        pr = e / jnp.sum(e, axis=-1, keepdims=True)
        o_h = _dot(pr, kv[:, D_MODEL + h * XA_DH:D_MODEL + (h + 1) * XA_DH])
        xa = xa + jnp.dot(o_h.astype(BF16), wo_ref[sl, :], preferred_element_type=F32)
    o_ref[...] = _layer_norm(DEEPNORM_ALPHA * x + xa, g_ref[...], b_ref[...])


def _xattn(x, kv, wq, wo, ln_g, ln_b, bsz, seq):
    n = x.shape[0]
    tm = min(512, seq)
    nt = seq // tm
    mem = kv.shape[1]
    full = lambda b, i: (0, 0)
    return pl.pallas_call(
        _xattn_kernel,
        grid=(bsz, nt),
        in_specs=[
            pl.BlockSpec((tm, D_MODEL), lambda b, i: (b * nt + i, 0)),
            pl.BlockSpec((1, mem, 2 * D_MODEL), lambda b, i: (b, 0, 0)),
            pl.BlockSpec((D_MODEL, D_MODEL), full), pl.BlockSpec((D_MODEL, D_MODEL), full),
            pl.BlockSpec((1, D_MODEL), full), pl.BlockSpec((1, D_MODEL), full),
        ],
        out_specs=pl.BlockSpec((tm, D_MODEL), lambda b, i: (b * nt + i, 0)),
        out_shape=jax.ShapeDtypeStruct((n, D_MODEL), F32),
        compiler_params=_cparams("parallel", "parallel"),
        name="xattn",
    )(x, kv, wq, wo, ln_g, ln_b)


R_ROWS = 8 + MOE_EXPERTS


def _router_kernel(x_ref, w_ref, b_ref, o_ref):
    tm = x_ref.shape[0]
    logits = lax.dot_general(w_ref[...], x_ref[...], (((1,), (1,)), ((), ())), precision=HIGHEST,
                             preferred_element_type=F32) + b_ref[...]
    sub = lax.broadcasted_iota(I32, (8, tm), 0)
    grp = jnp.where(sub < MOE_GROUPS, logits[0:8], NEG_BIG)
    gmax = jnp.max(grp, axis=0, keepdims=True)
    p_top = 1.0 / jnp.sum(jnp.exp(grp - gmax), axis=0, keepdims=True)
    g_sel = jnp.min(jnp.where(grp == gmax, sub, 8), axis=0, keepdims=True)
    sel = logits[8:16]
    for g in range(1, MOE_GROUPS):
        sel = jnp.where(g_sel == g, logits[8 + 8 * g:16 + 8 * g], sel)
    e = jnp.exp(sel - jnp.max(sel, axis=0, keepdims=True))
    p_in = e / jnp.sum(e, axis=0, keepdims=True)
    v1 = jnp.max(p_in, axis=0, keepdims=True)
    i1 = jnp.min(jnp.where(p_in == v1, sub, 8), axis=0, keepdims=True)
    rest = jnp.where(sub == i1, -1.0, p_in)
    v2 = jnp.max(rest, axis=0, keepdims=True)
    i2 = jnp.min(jnp.where(rest == v2, sub, 8), axis=0, keepdims=True)
    tot = v1 + v2
    base = g_sel * MOE_PER_GROUP
    rows = [(base + i1).astype(F32), (base + i2).astype(F32), v1 / tot * p_top, v2 / tot * p_top]
    out = jnp.zeros((8, tm), F32)
    for r, val in enumerate(rows):
        out = jnp.where(sub == r, val, out)
    o_ref[...] = out


def _router(x, w_rt, b_rt):
    n = x.shape[0]
    tm = min(512, n)
    return pl.pallas_call(
        _router_kernel,
        grid=(n // tm,),
        in_specs=[pl.BlockSpec((tm, D_MODEL), lambda i: (i, 0)),
                  pl.BlockSpec((R_ROWS, D_MODEL), lambda i: (0, 0)),
                  pl.BlockSpec((R_ROWS, 1), lambda i: (0, 0))],
        out_specs=pl.BlockSpec((8, tm), lambda i: (0, i)),
        out_shape=jax.ShapeDtypeStruct((8, n), F32),
        compiler_params=_cparams("parallel"),
        name="router",
    )(x, w_rt, b_rt)


def _slots_kernel(r_ref, dest_ref, meta_ref, cnt_ref, run_ref, pst_ref, *, tb, nmeta):
    ph = pl.program_id(0)
    i = pl.program_id(1)
    ex = lax.broadcasted_iota(I32, (MOE_EXPERTS, tb), 0).astype(F32)
    oh0 = (ex == r_ref[0:1, :]).astype(F32)
    oh1 = (ex == r_ref[1:2, :]).astype(F32)

    @pl.when((ph == 0) & (i == 0))
    def _():
        cnt_ref[...] = jnp.zeros_like(cnt_ref)

    @pl.when(ph == 0)
    def _():
        cnt_ref[...] += jnp.sum(oh0 + oh1, axis=1, keepdims=True)

    @pl.when((ph == 1) & (i == 0))
    def _():
        cnt = cnt_ref[...]
        padded = jnp.floor((cnt + (MOE_BLOCK - 1)) * (1.0 / MOE_BLOCK)) * MOE_BLOCK
        strict = _tri(MOE_EXPERTS, "strict").astype(F32)
        pstart = _dot_hi(strict, padded)
        pst_ref[...] = pstart
        run_ref[...] = jnp.zeros_like(run_ref)
        pend = (pstart + padded)[:, 0:1]
        blk0 = lax.broadcasted_iota(I32, (MOE_EXPERTS, nmeta), 1).astype(F32) * MOE_BLOCK
        be = jnp.sum((pend <= blk0).astype(F32), axis=0, keepdims=True)
        be = jnp.minimum(be, MOE_EXPERTS - 1.0)
        meta_ref[...] = jnp.broadcast_to(be, (8, nmeta)).astype(I32)

    @pl.when(ph == 1)
    def _():
        triu = _tri(tb, "upper").astype(BF16)
        cum0 = jnp.dot(oh0.astype(BF16), triu, preferred_element_type=F32)
        cum1 = jnp.dot(oh1.astype(BF16), triu, preferred_element_type=F32)
        tot0 = cum0[:, tb - 1:tb]
        tot1 = cum1[:, tb - 1:tb]
        base = run_ref[:, 0:1] + pst_ref[:, 0:1]
        d0 = jnp.sum(oh0 * (cum0 - 1.0 + base), axis=0, keepdims=True)
        d1 = jnp.sum(oh1 * (cum1 - 1.0 + base + tot0), axis=0, keepdims=True)
        run_ref[...] += tot0 + tot1
        sub = lax.broadcasted_iota(I32, (8, tb), 0)
        dest_ref[...] = jnp.where(sub == 0, d0, jnp.where(sub == 1, d1, 0.0)).astype(I32)


def _slots(route, n_blocks):
    n = route.shape[1]
    tb = min(256, n)
    nmeta = -(-n_blocks // LANES) * LANES
    return pl.pallas_call(
        functools.partial(_slots_kernel, tb=tb, nmeta=nmeta),
        grid=(2, n // tb),
        in_specs=[pl.BlockSpec((8, tb), lambda ph, i: (0, i))],
        out_specs=[pl.BlockSpec((8, tb), lambda ph, i: (0, i * ph)), pl.BlockSpec((8, nmeta), lambda ph, i: (0, 0))],
        out_shape=[jax.ShapeDtypeStruct((8, n), I32), jax.ShapeDtypeStruct((8, nmeta), I32)],
        scratch_shapes=[pltpu.VMEM((MOE_EXPERTS, LANES), F32), pltpu.VMEM((MOE_EXPERTS, LANES), F32),
                        pltpu.VMEM((MOE_EXPERTS, LANES), F32)],
        compiler_params=_cparams("arbitrary", "arbitrary"),
        name="moe_slots",
    )(route)


def _row_copy(src_ref, src_row, dst_ref, dst_row, sem):
    return pltpu.make_async_copy(src_ref.at[pl.ds(src_row, 1), :], dst_ref.at[pl.ds(dst_row, 1), :], sem)


def _dispatch_kernel(dest_ref, x_ref, xs_in_ref, xs_ref, sem, *, tb):
    del xs_in_ref

    def start(r, _):
        _row_copy(x_ref, r, xs_ref, dest_ref[0, r], sem).start()
        _row_copy(x_ref, r, xs_ref, dest_ref[1, r], sem).start()
        return 0

    def wait(r, _):
        _row_copy(x_ref, r, xs_ref, dest_ref[0, r], sem).wait()
        _row_copy(x_ref, r, xs_ref, dest_ref[1, r], sem).wait()
        return 0

    lax.fori_loop(0, tb, start, 0)
    lax.fori_loop(0, tb, wait, 0)


def _dispatch(dest, x, total):
    n = x.shape[0]
    tb = min(256, n)
    xs0 = jnp.zeros((total, D_MODEL), F32)
    return pl.pallas_call(
        functools.partial(_dispatch_kernel, tb=tb),
        grid=(n // tb,),
        in_specs=[pl.BlockSpec((8, tb), lambda i: (0, i), memory_space=pltpu.SMEM),
                  pl.BlockSpec((tb, D_MODEL), lambda i: (i, 0)),
                  pl.BlockSpec(memory_space=pl.ANY)],
        out_specs=pl.BlockSpec(memory_space=pl.ANY),
        out_shape=jax.ShapeDtypeStruct((total, D_MODEL), F32),
        scratch_shapes=[pltpu.SemaphoreType.DMA],
        input_output_aliases={2: 0},
        compiler_params=_cparams("arbitrary"),
        name="moe_dispatch",
    )(dest, x, xs0)


def _expert_kernel(be_ref, x_ref, wg_ref, wu_ref, wd_ref, o_ref):
    del be_ref
    x = x_ref[...].astype(BF16)
    gate = jnp.dot(x, wg_ref[0], preferred_element_type=F32)
    up = jnp.dot(x, wu_ref[0], preferred_element_type=F32)
    hidden = (_silu(gate) * up).astype(BF16)
    o_ref[...] = jnp.dot(hidden, wd_ref[0], preferred_element_type=F32)


def _experts(blk_expert, xs, w_gate, w_up, w_down):
    total = xs.shape[0]
    nb = total // MOE_BLOCK
    grid_spec = pltpu.PrefetchScalarGridSpec(
        num_scalar_prefetch=1,
        grid=(nb,),
        in_specs=[
            pl.BlockSpec((MOE_BLOCK, D_MODEL), lambda j, be: (j, 0)),
            pl.BlockSpec((1, D_MODEL, MOE_FF), lambda j, be: (be[j], 0, 0)),
            pl.BlockSpec((1, D_MODEL, MOE_FF), lambda j, be: (be[j], 0, 0)),
            pl.BlockSpec((1, MOE_FF, D_MODEL), lambda j, be: (be[j], 0, 0)),
        ],
        out_specs=pl.BlockSpec((MOE_BLOCK, D_MODEL), lambda j, be: (j, 0)),
    )
    return pl.pallas_call(
        _expert_kernel,
        grid_spec=grid_spec,
        out_shape=jax.ShapeDtypeStruct((total, D_MODEL), F32),
        compiler_params=_cparams("arbitrary"),
        name="moe_experts",
    )(blk_expert, xs, w_gate, w_up, w_down)


def _combine_kernel(dest_ref, r_ref, x_ref, ys_ref, g_ref, b_ref, o_ref, buf_ref, sem, *, tb):
    def start(r, _):
        _row_copy(ys_ref, dest_ref[0, r], buf_ref.at[0], r, sem).start()
        _row_copy(ys_ref, dest_ref[1, r], buf_ref.at[1], r, sem).start()
        return 0

    def wait(r, _):
        _row_copy(ys_ref, dest_ref[0, r], buf_ref.at[0], r, sem).wait()
        _row_copy(ys_ref, dest_ref[1, r], buf_ref.at[1], r, sem).wait()
        return 0

    lax.fori_loop(0, tb, start, 0)
    eye = (lax.broadcasted_iota(I32, (8, LANES), 0) == lax.broadcasted_iota(I32, (8, LANES), 1)).astype(F32)
    w_cols = _dot_tn_hi(r_ref[...], eye)
    lax.fori_loop(0, tb, wait, 0)
    y = w_cols[:, 2:3] * buf_ref[0] + w_cols[:, 3:4] * buf_ref[1]
    o_ref[...] = _layer_norm(DEEPNORM_ALPHA * x_ref[...] + y, g_ref[...], b_ref[...])


def _combine(dest, route, x, ys, ln_g, ln_b):
    n = x.shape[0]
    tb = min(256, n)
    return pl.pallas_call(
        functools.partial(_combine_kernel, tb=tb),
        grid=(n // tb,),
        in_specs=[pl.BlockSpec((8, tb), lambda i: (0, i), memory_space=pltpu.SMEM),
                  pl.BlockSpec((8, tb), lambda i: (0, i)),
                  pl.BlockSpec((tb, D_MODEL), lambda i: (i, 0)),
                  pl.BlockSpec(memory_space=pl.ANY),
                  pl.BlockSpec((1, D_MODEL), lambda i: (0, 0)), pl.BlockSpec((1, D_MODEL), lambda i: (0, 0))],
        out_specs=pl.BlockSpec((tb, D_MODEL), lambda i: (i, 0)),
        out_shape=jax.ShapeDtypeStruct((n, D_MODEL), F32),
        scratch_shapes=[pltpu.VMEM((2, tb, D_MODEL), F32), pltpu.SemaphoreType.DMA],
        compiler_params=_cparams("arbitrary"),
        name="moe_combine",
    )(dest, route, x, ys, ln_g, ln_b)


def _lane_vec(vals, offset, width=LANES):
    return jnp.zeros((1, width), F32).at[0, offset:offset + vals.shape[0]].set(vals.astype(F32))


def _pack_w_in(w_in):
    big = jnp.concatenate([w_in[:, 0:1536], w_in[:, 1544:2056], w_in[:, 2056:3592], w_in[:, 3600:5136],
                           w_in[:, 5152:8224]], axis=1).astype(BF16)
    small = jnp.zeros((D_MODEL, LANES), F32)
    small = small.at[:, G_B:G_B + 4].set(w_in[:, 1536:1540])
    small = small.at[:, G_A:G_A + 4].set(w_in[:, 1540:1544])
    small = small.at[:, G_F:G_F + 8].set(w_in[:, 3592:3600])
    small = small.at[:, G_LR:G_LR + 16].set(w_in[:, 5136:5152])
    return big, small


def _layer(x, mem_flat, bsz, seq, w):
    n = bsz * seq
    w_big, w_small = _pack_w_in(w["w_in"])
    proj_big = _matmul(x, w_big, BF16, 1024, 1024)
    gates = _matmul(x, w_small, F32, 1024, LANES, hi=True)

    cum_col, cum_row = _fox_cum(gates, _lane_vec(w["fox_f_bias"], G_F), bsz, seq)
    o_b = _fox_attention(proj_big, cum_col, cum_row, bsz, seq)
    o_a = _gdn(proj_big, gates, w["gdn_conv_w"], _lane_vec(w["gdn_a_log"], G_A), _lane_vec(w["gdn_dt_bias"], G_A),
               w["gdn_norm_w"].reshape(1, GDN_DV), bsz, seq)
    w2_pad = jnp.zeros((LANES, GLA_QK), F32).at[G_LR:G_LR + GLA_RANK].set(w["gla_w_gate2"])
    o_c = _gla(proj_big, gates, w2_pad, w["gla_b_gate"].reshape(1, GLA_QK), w["gla_norm_w"].reshape(1, GLA_DV),
               bsz, seq)
    x = _merge(o_a, o_b, o_c, proj_big, x, w["p_gdn"].astype(BF16), w["p_fox"].astype(BF16),
               w["p_gla"].astype(BF16), w["b_merge"].reshape(N_BRANCH, D_MODEL), w["w_out"].astype(BF16),
               w["ln1_g"].reshape(1, D_MODEL), w["ln1_b"].reshape(1, D_MODEL))

    mem_len = mem_flat.shape[0] // bsz
    kv = _matmul(mem_flat, w["xa_wkv"].astype(BF16), BF16, 512, 1024).reshape(bsz, mem_len, 2 * D_MODEL)
    x = _xattn(x, kv, w["xa_wq"].astype(BF16), w["xa_wo"].astype(BF16), w["ln2_g"].reshape(1, D_MODEL),
               w["ln2_b"].reshape(1, D_MODEL), bsz, seq)

    w_rt = jnp.zeros((R_ROWS, D_MODEL), F32).at[0:MOE_GROUPS].set(w["moe_w_group"].T).at[8:].set(w["moe_w_expert"].T)
    b_rt = jnp.zeros((R_ROWS, 1), F32).at[0:MOE_GROUPS, 0].set(w["moe_b_group"]).at[8:, 0].set(w["moe_b_expert"])
    route = _router(x, w_rt, b_rt)
    nk = 2 * n
    total = -(-nk // MOE_BLOCK) * MOE_BLOCK + MOE_EXPERTS * MOE_BLOCK
    n_blocks = total // MOE_BLOCK
    dest, meta = _slots(route, n_blocks)
    xs = _dispatch(dest, x, total)
    ys = _experts(meta[0, :n_blocks], xs, w["moe_w_gate"].astype(BF16), w["moe_w_up"].astype(BF16),
                  w["moe_w_down"].astype(BF16))
    return _combine(dest, route, x, ys, w["ln3_g"].reshape(1, D_MODEL), w["ln3_b"].reshape(1, D_MODEL))


_PARAM_NAMES = ("w_in", "gdn_conv_w", "gdn_a_log", "gdn_dt_bias", "gdn_norm_w", "fox_f_bias", "gla_w_gate2",
                "gla_b_gate", "gla_norm_w", "p_gdn", "p_fox", "p_gla", "b_merge", "w_out", "ln1_g", "ln1_b",
                "xa_wq", "xa_wkv", "xa_wo", "ln2_g", "ln2_b", "moe_w_group", "moe_b_group", "moe_w_expert",
                "moe_b_expert", "moe_w_gate", "moe_w_up", "moe_w_down", "ln3_g", "ln3_b")


def kernel(x, mem, w_in, gdn_conv_w, gdn_a_log, gdn_dt_bias, gdn_norm_w, fox_f_bias, gla_w_gate2, gla_b_gate, gla_norm_w, p_gdn, p_fox, p_gla, b_merge, w_out, ln1_g, ln1_b, xa_wq, xa_wkv, xa_wo, ln2_g, ln2_b, moe_w_group, moe_b_group, moe_w_expert, moe_b_expert, moe_w_gate, moe_w_up, moe_w_down, ln3_g, ln3_b):
    params = (w_in, gdn_conv_w, gdn_a_log, gdn_dt_bias, gdn_norm_w, fox_f_bias, gla_w_gate2, gla_b_gate, gla_norm_w,
              p_gdn, p_fox, p_gla, b_merge, w_out, ln1_g, ln1_b, xa_wq, xa_wkv, xa_wo, ln2_g, ln2_b, moe_w_group,
              moe_b_group, moe_w_expert, moe_b_expert, moe_w_gate, moe_w_up, moe_w_down, ln3_g, ln3_b)
    bsz, seq, d = x.shape
    h = x.reshape(bsz * seq, d)
    mem_flat = mem.reshape(bsz * mem.shape[1], d)
    for l in range(w_in.shape[0]):
        h = _layer(h, mem_flat, bsz, seq, {name: p[l] for name, p in zip(_PARAM_NAMES, params)})
    return h.reshape(bsz, seq, d)
```

```python
import functools

import jax
import jax.numpy as jnp
from jax import lax
from jax.experimental import pallas as pl
from jax.experimental.pallas import tpu as pltpu

F32 = jnp.float32
BF16 = jnp.bfloat16
I32 = jnp.int32
HIGHEST = lax.Precision.HIGHEST

D_MODEL = 1024
DEPTH = 2
GDN_HEADS = 4
GDN_DK = 128
GDN_DV = 128
GDN_QK = GDN_HEADS * GDN_DK
GDN_V = GDN_HEADS * GDN_DV
GDN_CONV = 4
CHUNK = 64
FOX_HEADS = 8
FOX_DH = 64
FOX_W = FOX_HEADS * FOX_DH
GLA_HEADS = 4
GLA_DK = 64
GLA_DV = 128
GLA_QK = GLA_HEADS * GLA_DK
GLA_V = GLA_HEADS * GLA_DV
GLA_RANK = 16
GLA_TAU = 16.0
N_BRANCH = 3
XA_HEADS = 4
XA_DH = D_MODEL // XA_HEADS
MOE_GROUPS = 4
MOE_PER_GROUP = 8
MOE_EXPERTS = MOE_GROUPS * MOE_PER_GROUP
MOE_FF = D_MODEL // 4
MOE_BLOCK = 256
DEEPNORM_ALPHA = (2 * DEPTH) ** 0.25
LN_EPS = 1e-5
RMS_EPS = 1e-6

LANES = 128
CONV_PAD = 16
NEG_BIG = -1e30
VMEM_LIMIT = 56 * 1024 * 1024

C_GDN_QKV = 0
C_GDN_Z = 1536
C_FOX_Q = 2048
C_FOX_K = 2560
C_FOX_V = 3072
C_GLA_Q = 3584
C_GLA_K = 3840
C_GLA_V = 4096
C_GLA_R = 4608
C_MERGE = 5120
N_BIG = 8192
G_B, G_A, G_F, G_LR = 0, 4, 8, 16


def _cparams(*sem):
    return pltpu.CompilerParams(dimension_semantics=sem, vmem_limit_bytes=VMEM_LIMIT)


def _sigmoid(x):
    return 1.0 / (1.0 + jnp.exp(-x))


def _silu(x):
    return x * _sigmoid(x)


def _softplus(x):
    return jnp.maximum(x, 0.0) + jnp.log(1.0 + jnp.exp(-jnp.abs(x)))


def _log_sigmoid(x):
    return jnp.minimum(x, 0.0) - jnp.log(1.0 + jnp.exp(-jnp.abs(x)))


def _layer_norm(y, g, b):
    mu = jnp.mean(y, axis=-1, keepdims=True)
    yc = y - mu
    var = jnp.mean(yc * yc, axis=-1, keepdims=True)
    return yc * lax.rsqrt(var + LN_EPS) * g + b


def _dot(a, b):
    return jnp.dot(a.astype(BF16), b.astype(BF16), preferred_element_type=F32)


def _dot_nt(a, b):
    return lax.dot_general(a.astype(BF16), b.astype(BF16), (((1,), (1,)), ((), ())),
                           preferred_element_type=F32)


def _dot_tn(a, b):
    return lax.dot_general(a.astype(BF16), b.astype(BF16), (((0,), (0,)), ((), ())),
                           preferred_element_type=F32)


def _dot_hi(a, b):
    return jnp.dot(a, b, precision=HIGHEST, preferred_element_type=F32)


def _dot_tn_hi(a, b):
    return lax.dot_general(a, b, (((0,), (0,)), ((), ())), precision=HIGHEST, preferred_element_type=F32)


def _tri(n, kind):
    r = lax.broadcasted_iota(I32, (n, n), 0)
    c = lax.broadcasted_iota(I32, (n, n), 1)
    if kind == "lower":
        return r >= c
    if kind == "strict":
        return r > c
    return r <= c


def _mm_kernel(a_ref, b_ref, o_ref):
    o_ref[...] = jnp.dot(a_ref[...].astype(BF16), b_ref[...], preferred_element_type=F32).astype(o_ref.dtype)


def _mm_hi_kernel(a_ref, b_ref, o_ref):
    o_ref[...] = _dot_hi(a_ref[...].astype(F32), b_ref[...]).astype(o_ref.dtype)


def _matmul(a, b, out_dtype, tm, tn, hi=False):
    m, k = a.shape
    n = b.shape[1]
    tm, tn = min(tm, m), min(tn, n)
    return pl.pallas_call(
        _mm_hi_kernel if hi else _mm_kernel,
        grid=(m // tm, n // tn),
        in_specs=[pl.BlockSpec((tm, k), lambda i, j: (i, 0)), pl.BlockSpec((k, tn), lambda i, j: (0, j))],
        out_specs=pl.BlockSpec((tm, tn), lambda i, j: (i, j)),
        out_shape=jax.ShapeDtypeStruct((m, n), out_dtype),
        compiler_params=_cparams("parallel", "parallel"),
        name="proj_hi" if hi else "proj",
    )(a, b)


def _fcum_kernel(g_ref, fb_ref, row_ref, *, seq, blk):
    triu = _tri(blk, "upper").astype(F32)
    carry = jnp.zeros((LANES, 1), F32)
    for i in range(seq // blk):
        lf = _log_sigmoid(g_ref[i * blk:(i + 1) * blk, :] + fb_ref[...])
        cum_t = _dot_tn_hi(lf, triu) + carry
        row_ref[0, :, i * blk:(i + 1) * blk] = cum_t[G_F:G_F + FOX_HEADS, :]
        carry = cum_t[:, blk - 1:blk]


def _fox_cum(gates, fb_vec, bsz, seq):
    blk = min(256, seq)
    return pl.pallas_call(
        functools.partial(_fcum_kernel, seq=seq, blk=blk),
        grid=(bsz,),
        in_specs=[pl.BlockSpec((seq, LANES), lambda b: (b, 0)), pl.BlockSpec((1, LANES), lambda b: (0, 0))],
        out_specs=pl.BlockSpec((1, FOX_HEADS, seq), lambda b: (b, 0, 0)),
        out_shape=jax.ShapeDtypeStruct((bsz, FOX_HEADS, seq), F32),
        compiler_params=_cparams("parallel"),
        name="fox_cum",
    )(gates, fb_vec)


FOX_GROUP = 4


def _fox_kernel(q_ref, k_ref, v_ref, cr_ref, o_ref, va_ref, *, tq, seq):
    grp = pl.program_id(1)
    qi = pl.program_id(2)
    lane = lax.broadcasted_iota(I32, (tq, LANES), 1)
    diag = lax.broadcasted_iota(I32, (tq, tq), 1) <= lax.broadcasted_iota(I32, (tq, tq), 0)
    scale = jnp.asarray(FOX_DH ** -0.5, BF16)

    def head_lanes(g, rows):
        ln = lax.broadcasted_iota(I32, (rows, LANES), 1)
        return (ln >= FOX_DH * (g % 2)) & (ln < FOX_DH * (g % 2 + 1))

    @pl.when(qi == 0)
    def _():
        for g in range(FOX_GROUP):
            vp = v_ref[:, (g // 2) * LANES:(g // 2 + 1) * LANES]
            va_ref[g] = jnp.where(head_lanes(g, seq), vp, jnp.ones_like(vp))

    qs = []
    for g in range(FOX_GROUP):
        qp = q_ref[:, (g // 2) * LANES:(g // 2 + 1) * LANES]
        qs.append(jnp.where(head_lanes(g, tq), qp, jnp.zeros_like(qp)) * scale)

    def tile(j, carry, masked):
        start = pl.multiple_of(j * tq, tq)
        out = []
        for g in range(FOX_GROUP):
            m, acc = carry[g]
            ps = slice((g // 2) * LANES, (g // 2 + 1) * LANES)
            c_k = cr_ref[0, pl.ds(FOX_GROUP * grp + g, 1), pl.ds(start, tq)]
            s = _dot_nt(qs[g], k_ref[pl.ds(start, tq), ps]) - c_k
            if masked:
                s = jnp.where(diag, s, NEG_BIG)
            m_new = jnp.maximum(m, jnp.max(s, axis=1, keepdims=True))
            pr = jnp.exp(s - m_new).astype(BF16)
            acc = jnp.exp(m - m_new) * acc + jnp.dot(pr, va_ref[g, pl.ds(start, tq), :], preferred_element_type=F32)
            out.append((m_new, acc))
        return tuple(out)

    init = tuple((jnp.full((tq, 1), NEG_BIG, F32), jnp.zeros((tq, LANES), F32)) for _ in range(FOX_GROUP))
    carry = lax.fori_loop(0, qi, lambda j, cy: tile(j, cy, False), init)
    carry = tile(qi, carry, True)
    res = [acc / pltpu.roll(acc, FOX_DH, axis=1) for _, acc in carry]
    for p in range(FOX_GROUP // 2):
        o_ref[:, p * LANES:(p + 1) * LANES] = jnp.where(lane < FOX_DH, res[2 * p], res[2 * p + 1]).astype(o_ref.dtype)


def _fox_attention(proj_big, cum_row, bsz, seq):
    tq = min(256, seq)
    nq = seq // tq
    n = bsz * seq
    gw = FOX_GROUP * FOX_DH
    qb, kb, vb = C_FOX_Q // gw, C_FOX_K // gw, C_FOX_V // gw
    return pl.pallas_call(
        functools.partial(_fox_kernel, tq=tq, seq=seq),
        grid=(bsz, FOX_HEADS // FOX_GROUP, nq),
        in_specs=[
            pl.BlockSpec((tq, gw), lambda b, p, i: (b * nq + i, qb + p)),
            pl.BlockSpec((seq, gw), lambda b, p, i: (b, kb + p)),
            pl.BlockSpec((seq, gw), lambda b, p, i: (b, vb + p)),
            pl.BlockSpec((1, FOX_HEADS, seq), lambda b, p, i: (b, 0, 0)),
        ],
        out_specs=pl.BlockSpec((tq, gw), lambda b, p, i: (b * nq + i, p)),
        out_shape=jax.ShapeDtypeStruct((n, FOX_W), BF16),
        scratch_shapes=[pltpu.VMEM((FOX_GROUP, seq, LANES), BF16)],
        compiler_params=_cparams("parallel", "parallel", "arbitrary"),
        name="fox_attn",
    )(proj_big, proj_big, proj_big, cum_row)


def _unit_lower_inverse(low, nil):
    n = low.shape[0]
    eye = (lax.broadcasted_iota(I32, (n, n), 0) == lax.broadcasted_iota(I32, (n, n), 1)).astype(F32)
    inv = eye - low
    power = _dot(low, low)
    span = 2
    while span < nil:
        inv = inv + _dot(inv, power)
        span *= 2
        if span < nil:
            power = _dot(power, power)
    return inv


GDN_GROUP = 4


def _gdn_kernel(qkv_ref, z_ref, g_ref, cw_ref, nega_ref, dt_ref, nw_ref, o_ref,
                pad_ref, u_ref, wq_ref, kd_ref, at_ref, cd_ref, st_ref, *, seq):
    c = CHUNK
    n_chunks = seq // c
    gr_rows = GDN_GROUP * c
    pad_ref[0:CONV_PAD, :] = jnp.zeros((CONV_PAD, 3 * GDN_QK), BF16)
    pad_ref[CONV_PAD:CONV_PAD + seq, :] = qkv_ref[...]
    st_ref[...] = jnp.zeros_like(st_ref)
    row = lax.broadcasted_iota(I32, (gr_rows, gr_rows), 0)
    col = lax.broadcasted_iota(I32, (gr_rows, gr_rows), 1)
    same = (row | (c - 1)) == (col | (c - 1))
    causal = same & (row >= col)
    strict = same & (row > col)
    tril = causal.astype(F32)
    triu = (same & (row <= col)).astype(F32)
    last = (col == (row | (c - 1))).astype(F32)
    cw = cw_ref[...]
    neg_a = -jnp.exp(nega_ref[...])
    lane1 = lax.broadcasted_iota(I32, (1, LANES), 1)
    neg_a = jnp.where((lane1 >= G_A) & (lane1 < G_A + GDN_HEADS), neg_a, 0.0)
    dt = dt_ref[...]
    nw = nw_ref[...]
    sub8 = lax.broadcasted_iota(I32, (8, LANES), 0)

    def local_group(gi, _):
        r0 = pl.multiple_of(gi * gr_rows, gr_rows)
        win = pad_ref[pl.ds(r0, gr_rows + CONV_PAD), :]
        gs = g_ref[pl.ds(r0, gr_rows), :]
        beta_all = _sigmoid(gs)
        g_all = neg_a * _softplus(gs + dt)
        cum = _dot_hi(tril, g_all)
        cum_t = _dot_tn_hi(g_all, triu)
        cum_last = _dot_hi(last, cum)
        e_last = jnp.exp(cum_last)

        def conv(col0):
            xw = win[:, col0:col0 + LANES].astype(F32)
            acc = xw[CONV_PAD - 3:CONV_PAD - 3 + gr_rows] * cw[0:1, col0:col0 + LANES]
            for i in range(1, GDN_CONV):
                acc = acc + xw[CONV_PAD - 3 + i:CONV_PAD - 3 + i + gr_rows] * cw[i:i + 1, col0:col0 + LANES]
            return _silu(acc)

        def l2n(x):
            return x * lax.rsqrt(jnp.sum(x * x, axis=-1, keepdims=True) + RMS_EPS)

        cd_tiles = [jnp.zeros((8, LANES), F32) for _ in range(GDN_GROUP)]
        for h in range(GDN_HEADS):
            hs = slice(h * GDN_DV, (h + 1) * GDN_DV)
            q = l2n(conv(h * GDN_DK)) * (GDN_DK ** -0.5)
            k = l2n(conv(GDN_QK + h * GDN_DK))
            v = conv(2 * GDN_QK + h * GDN_DV)
            beta = beta_all[:, G_B + h:G_B + h + 1]
            gc = cum[:, G_A + h:G_A + h + 1]
            gr = cum_t[G_A + h:G_A + h + 1, :]
            gl = cum_last[:, G_A + h:G_A + h + 1]
            eg = jnp.exp(gc)
            decay = jnp.where(causal, jnp.exp(jnp.where(causal, gc - gr, 0.0)), 0.0)
            kb = k * beta
            low = jnp.where(strict, _dot_nt(kb, k) * decay, 0.0)
            t_inv = _unit_lower_inverse(low, c)
            uw = _dot(t_inv, jnp.concatenate([v * beta, kb * eg], axis=1))
            attn = jnp.where(causal, _dot_nt(q, k) * decay, 0.0).astype(BF16)
            qd = (q * eg).astype(BF16)
            u_ref[pl.ds(r0, gr_rows), hs] = uw[:, 0:GDN_DV]
            kd_ref[pl.ds(r0, gr_rows), hs] = (k * jnp.exp(gl - gc)).astype(BF16)
            w = uw[:, GDN_DV:2 * GDN_DV].astype(BF16)
            for j in range(GDN_GROUP):
                js = slice(j * c, (j + 1) * c)
                wq_ref[pl.ds(2 * r0 + 2 * j * c, c), hs] = w[js]
                wq_ref[pl.ds(2 * r0 + 2 * j * c + c, c), hs] = qd[js]
                at_ref[h, pl.ds(r0 + j * c, c), :] = attn[js, js]
                cd_tiles[j] = jnp.where(sub8 == h, e_last[j * c:j * c + 1, G_A + h:G_A + h + 1], cd_tiles[j])
        for j in range(GDN_GROUP):
            cd_ref[pl.ds(pl.multiple_of(gi * (8 * GDN_GROUP), 8 * GDN_GROUP) + 8 * j, 8), :] = cd_tiles[j]
        return 0

    def recurrent_chunk(ci, _):
        r0 = pl.multiple_of(ci * c, c)
        z = z_ref[pl.ds(r0, c), :].astype(F32)
        cd_tile = cd_ref[pl.ds(pl.multiple_of(ci * 8, 8), 8), :]
        outs = []
        for h in range(GDN_HEADS):
            hs = slice(h * GDN_DV, (h + 1) * GDN_DV)
            state = st_ref[h]
            ws = jnp.dot(wq_ref[pl.ds(pl.multiple_of(2 * r0, 2 * c), 2 * c), hs], state.astype(BF16),
                         preferred_element_type=F32)
            v_new = (u_ref[pl.ds(r0, c), hs] - ws[0:c]).astype(BF16)
            out = ws[c:2 * c] + jnp.dot(at_ref[h, pl.ds(r0, c), :], v_new, preferred_element_type=F32)
            st_ref[h] = state * cd_tile[h:h + 1, :] + _dot_tn(kd_ref[pl.ds(r0, c), hs], v_new)
            rms = out * lax.rsqrt(jnp.mean(out * out, axis=-1, keepdims=True) + RMS_EPS) * nw
            outs.append(rms * _silu(z[:, hs]))
        o_ref[pl.ds(r0, c), :] = jnp.concatenate(outs, axis=1).astype(o_ref.dtype)
        return 0

    lax.fori_loop(0, n_chunks // GDN_GROUP, local_group, 0)
    lax.fori_loop(0, n_chunks, recurrent_chunk, 0)


def _gdn(proj_big, gates, conv_w, a_vec, dt_vec, norm_w, bsz, seq):
    n = bsz * seq
    return pl.pallas_call(
        functools.partial(_gdn_kernel, seq=seq),
        grid=(bsz,),
        in_specs=[
            pl.BlockSpec((seq, 3 * GDN_QK), lambda b: (b, 0)),
            pl.BlockSpec((seq, GDN_V), lambda b: (b, C_GDN_Z // GDN_V)),
            pl.BlockSpec((seq, LANES), lambda b: (b, 0)),
            pl.BlockSpec((GDN_CONV, 3 * GDN_QK), lambda b: (0, 0)),
            pl.BlockSpec((1, LANES), lambda b: (0, 0)),
            pl.BlockSpec((1, LANES), lambda b: (0, 0)),
            pl.BlockSpec((1, GDN_DV), lambda b: (0, 0)),
        ],
        out_specs=pl.BlockSpec((seq, GDN_V), lambda b: (b, 0)),
        out_shape=jax.ShapeDtypeStruct((n, GDN_V), BF16),
        scratch_shapes=[pltpu.VMEM((seq + CONV_PAD, 3 * GDN_QK), BF16),
                        pltpu.VMEM((seq, GDN_V), F32),
                        pltpu.VMEM((2 * seq, GDN_QK), BF16),
                        pltpu.VMEM((seq, GDN_QK), BF16),
                        pltpu.VMEM((GDN_HEADS, seq, CHUNK), BF16),
                        pltpu.VMEM((seq // CHUNK * 8, LANES), F32),
                        pltpu.VMEM((GDN_HEADS, GDN_DK, GDN_DV), F32)],
        compiler_params=_cparams("parallel"),
        name="gdn",
    )(proj_big, proj_big, gates, conv_w, a_vec, dt_vec, norm_w)


def _gla_kernel(q_ref, k_ref, v_ref, r_ref, g_ref, w2_ref, bg_ref, nw_ref, o_ref, st_ref, *, seq):
    c = CHUNK
    st_ref[...] = jnp.zeros_like(st_ref)
    causal = _tri(c, "lower")
    tril = causal.astype(F32)
    w2 = w2_ref[...]
    bg = bg_ref[...]
    nw = nw_ref[...]
    lane1 = lax.broadcasted_iota(I32, (1, LANES), 1)

    def chunk_body(ci, _):
        r0 = pl.multiple_of(ci * c, c)
        gs = g_ref[pl.ds(r0, c), :]
        log_a = _log_sigmoid(_dot_hi(gs, w2) + bg) * (1.0 / GLA_TAU)
        cum = _dot_hi(tril, log_a)
        clast = cum[c - 1:c, :]
        q = q_ref[pl.ds(r0, c), :].astype(F32)
        k = k_ref[pl.ds(r0, c), :].astype(F32)
        v = v_ref[pl.ds(r0, c), :]
        r = r_ref[pl.ds(r0, c), :].astype(F32)
        q_t = q * jnp.exp(cum) * (GLA_DK ** -0.5)
        k_t = k * jnp.exp(-cum)
        k_dec = k * jnp.exp(clast - cum)
        cdec = jnp.exp(clast)
        outs = []
        for h in range(GLA_HEADS):
            p, hh = divmod(h, 2)
            sl = slice(p * LANES, (p + 1) * LANES)
            head_lanes = (lane1 >= GLA_DK * hh) & (lane1 < GLA_DK * (hh + 1))
            qh = jnp.where(head_lanes, q_t[:, sl], 0.0)
            attn = jnp.where(causal, _dot_nt(qh, k_t[:, sl]), 0.0)
            vh = v[:, h * GLA_DV:(h + 1) * GLA_DV]
            state_t = st_ref[h]
            out = _dot(attn, vh) + _dot_nt(qh, state_t)
            st_ref[h] = state_t * cdec[:, sl] + _dot_tn(vh, jnp.where(head_lanes, k_dec[:, sl], 0.0))
            rms = out * lax.rsqrt(jnp.mean(out * out, axis=-1, keepdims=True) + RMS_EPS) * nw
            outs.append(rms * _silu(r[:, h * GLA_DV:(h + 1) * GLA_DV]))
        o_ref[pl.ds(r0, c), :] = jnp.concatenate(outs, axis=1).astype(o_ref.dtype)
        return 0

    lax.fori_loop(0, seq // c, chunk_body, 0, unroll=2)


def _gla(proj_big, gates, w2_pad, b_gate, norm_w, bsz, seq):
    n = bsz * seq
    return pl.pallas_call(
        functools.partial(_gla_kernel, seq=seq),
        grid=(bsz,),
        in_specs=[
            pl.BlockSpec((seq, GLA_QK), lambda b: (b, C_GLA_Q // GLA_QK)),
            pl.BlockSpec((seq, GLA_QK), lambda b: (b, C_GLA_K // GLA_QK)),
            pl.BlockSpec((seq, GLA_V), lambda b: (b, C_GLA_V // GLA_V)),
            pl.BlockSpec((seq, GLA_V), lambda b: (b, C_GLA_R // GLA_V)),
            pl.BlockSpec((seq, LANES), lambda b: (b, 0)),
            pl.BlockSpec((LANES, GLA_QK), lambda b: (0, 0)),
            pl.BlockSpec((1, GLA_QK), lambda b: (0, 0)),
            pl.BlockSpec((1, GLA_DV), lambda b: (0, 0)),
        ],
        out_specs=pl.BlockSpec((seq, GLA_V), lambda b: (b, 0)),
        out_shape=jax.ShapeDtypeStruct((n, GLA_V), BF16),
        scratch_shapes=[pltpu.VMEM((GLA_HEADS, GLA_DV, LANES), F32)],
        compiler_params=_cparams("parallel"),
        name="gla",
    )(proj_big, proj_big, proj_big, proj_big, gates, w2_pad, b_gate, norm_w)


def _merge_kernel(oa_ref, ob_ref, oc_ref, m0_ref, m1_ref, m2_ref, x_ref, pa_ref, pb_ref, pc_ref, bm_ref,
                  wo_ref, g_ref, b_ref, o_ref):
    bm = bm_ref[...]
    merged = _sigmoid(m0_ref[...].astype(F32) + bm[0:1]) * jnp.dot(oa_ref[...], pa_ref[...],
                                                                   preferred_element_type=F32)
    merged += _sigmoid(m1_ref[...].astype(F32) + bm[1:2]) * jnp.dot(ob_ref[...], pb_ref[...],
                                                                    preferred_element_type=F32)
    merged += _sigmoid(m2_ref[...].astype(F32) + bm[2:3]) * jnp.dot(oc_ref[...], pc_ref[...],
                                                                    preferred_element_type=F32)
    mix = jnp.dot(merged.astype(BF16), wo_ref[...], preferred_element_type=F32)
    o_ref[...] = _layer_norm(DEEPNORM_ALPHA * x_ref[...] + mix, g_ref[...], b_ref[...])


def _merge(o_a, o_b, o_c, proj_big, x, p_a, p_b, p_c, b_merge, w_out, ln_g, ln_b):
    n = x.shape[0]
    tm = min(512, n)
    mb = C_MERGE // D_MODEL
    row = lambda i: (i, 0)
    full = lambda i: (0, 0)
    return pl.pallas_call(
        _merge_kernel,
        grid=(n // tm,),
        in_specs=[
            pl.BlockSpec((tm, GDN_V), row), pl.BlockSpec((tm, FOX_W), row), pl.BlockSpec((tm, GLA_V), row),
            pl.BlockSpec((tm, D_MODEL), lambda i: (i, mb)),
            pl.BlockSpec((tm, D_MODEL), lambda i: (i, mb + 1)),
            pl.BlockSpec((tm, D_MODEL), lambda i: (i, mb + 2)),
            pl.BlockSpec((tm, D_MODEL), row),
            pl.BlockSpec((GDN_V, D_MODEL), full), pl.BlockSpec((FOX_W, D_MODEL), full),
            pl.BlockSpec((GLA_V, D_MODEL), full),
            pl.BlockSpec((N_BRANCH, D_MODEL), full),
            pl.BlockSpec((D_MODEL, D_MODEL), full),
            pl.BlockSpec((1, D_MODEL), full), pl.BlockSpec((1, D_MODEL), full),
        ],
        out_specs=pl.BlockSpec((tm, D_MODEL), row),
        out_shape=jax.ShapeDtypeStruct((n, D_MODEL), F32),
        compiler_params=_cparams("parallel"),
        name="merge_out",
    )(o_a, o_b, o_c, proj_big, proj_big, proj_big, x, p_a, p_b, p_c, b_merge, w_out, ln_g, ln_b)


TOK_ROWS = D_MODEL // LANES


def _store_token_tiles(t_ref, y):
    m = y.shape[0]
    for s in range(TOK_ROWS):
        t_ref[pl.ds(s, m, stride=TOK_ROWS), :] = y[:, s * LANES:(s + 1) * LANES]


def _load_token_tiles(t_ref, m):
    return jnp.concatenate([t_ref[pl.ds(s, m, stride=TOK_ROWS), :] for s in range(TOK_ROWS)], axis=1)


def _xattn_kernel(x_ref, kv_ref, wq_ref, wo_ref, g_ref, b_ref, o_ref, t_ref):
    x = x_ref[...]
    q = jnp.dot(x.astype(BF16), wq_ref[...], preferred_element_type=F32).astype(BF16)
    kv = kv_ref[0]
    xa = jnp.zeros(x.shape, F32)
    for h in range(XA_HEADS):
        sl = slice(h * XA_DH, (h + 1) * XA_DH)
        s = _dot_nt(q[:, sl], kv[:, sl]) * (XA_DH ** -0.5)
        s = s - jnp.max(s, axis=-1, keepdims=True)
        e = jnp.exp(s)
        pr = e / jnp.sum(e, axis=-1, keepdims=True)
        o_h = _dot(pr, kv[:, D_MODEL + h * XA_DH:D_MODEL + (h + 1) * XA_DH])
        xa = xa + jnp.dot(o_h.astype(BF16), wo_ref[sl, :], preferred_element_type=F32)
    y = _layer_norm(DEEPNORM_ALPHA * x + xa, g_ref[...], b_ref[...])
    o_ref[...] = y
    _store_token_tiles(t_ref, y)


def _xattn(x, kv, wq, wo, ln_g, ln_b, bsz, seq):
    n = x.shape[0]
    tm = min(512, seq)
    nt = seq // tm
    mem = kv.shape[1]
    full = lambda b, i: (0, 0)
    return pl.pallas_call(
        _xattn_kernel,
        grid=(bsz, nt),
        in_specs=[
            pl.BlockSpec((tm, D_MODEL), lambda b, i: (b * nt + i, 0)),
            pl.BlockSpec((1, mem, 2 * D_MODEL), lambda b, i: (b, 0, 0)),
            pl.BlockSpec((D_MODEL, D_MODEL), full), pl.BlockSpec((D_MODEL, D_MODEL), full),
            pl.BlockSpec((1, D_MODEL), full), pl.BlockSpec((1, D_MODEL), full),
        ],
        out_specs=[pl.BlockSpec((tm, D_MODEL), lambda b, i: (b * nt + i, 0)),
                   pl.BlockSpec((tm * TOK_ROWS, LANES), lambda b, i: (b * nt + i, 0))],
        out_shape=[jax.ShapeDtypeStruct((n, D_MODEL), F32), jax.ShapeDtypeStruct((n * TOK_ROWS, LANES), F32)],
        compiler_params=_cparams("parallel", "parallel"),
        name="xattn",
    )(x, kv, wq, wo, ln_g, ln_b)


R_ROWS = 8 + MOE_EXPERTS


def _router_kernel(x_ref, w_ref, b_ref, o_ref):
    tm = x_ref.shape[0]
    logits = lax.dot_general(w_ref[...], x_ref[...], (((1,), (1,)), ((), ())), precision=HIGHEST,
                             preferred_element_type=F32) + b_ref[...]
    sub = lax.broadcasted_iota(I32, (8, tm), 0)
    grp = jnp.where(sub < MOE_GROUPS, logits[0:8], NEG_BIG)
    gmax = jnp.max(grp, axis=0, keepdims=True)
    p_top = 1.0 / jnp.sum(jnp.exp(grp - gmax), axis=0, keepdims=True)
    g_sel = jnp.min(jnp.where(grp == gmax, sub, 8), axis=0, keepdims=True)
    sel = logits[8:16]
    for g in range(1, MOE_GROUPS):
        sel = jnp.where(g_sel == g, logits[8 + 8 * g:16 + 8 * g], sel)
    e = jnp.exp(sel - jnp.max(sel, axis=0, keepdims=True))
    p_in = e / jnp.sum(e, axis=0, keepdims=True)
    v1 = jnp.max(p_in, axis=0, keepdims=True)
    i1 = jnp.min(jnp.where(p_in == v1, sub, 8), axis=0, keepdims=True)
    rest = jnp.where(sub == i1, -1.0, p_in)
    v2 = jnp.max(rest, axis=0, keepdims=True)
    i2 = jnp.min(jnp.where(rest == v2, sub, 8), axis=0, keepdims=True)
    tot = v1 + v2
    base = g_sel * MOE_PER_GROUP
    rows = [(base + i1).astype(F32), (base + i2).astype(F32), v1 / tot * p_top, v2 / tot * p_top]
    out = jnp.zeros((8, tm), F32)
    for r, val in enumerate(rows):
        out = jnp.where(sub == r, val, out)
    o_ref[...] = out


def _router(x, w_rt, b_rt):
    n = x.shape[0]
    tm = min(512, n)
    return pl.pallas_call(
        _router_kernel,
        grid=(n // tm,),
        in_specs=[pl.BlockSpec((tm, D_MODEL), lambda i: (i, 0)),
                  pl.BlockSpec((R_ROWS, D_MODEL), lambda i: (0, 0)),
                  pl.BlockSpec((R_ROWS, 1), lambda i: (0, 0))],
        out_specs=pl.BlockSpec((8, tm), lambda i: (0, i)),
        out_shape=jax.ShapeDtypeStruct((8, n), F32),
        compiler_params=_cparams("parallel"),
        name="router",
    )(x, w_rt, b_rt)


def _slots_kernel(r_ref, dest_ref, meta_ref, cnt_ref, run_ref, pst_ref, *, tb, nmeta):
    ph = pl.program_id(0)
    i = pl.program_id(1)
    ex = lax.broadcasted_iota(I32, (MOE_EXPERTS, tb), 0).astype(F32)
    oh0 = (ex == r_ref[0:1, :]).astype(F32)
    oh1 = (ex == r_ref[1:2, :]).astype(F32)

    @pl.when((ph == 0) & (i == 0))
    def _():
        cnt_ref[...] = jnp.zeros_like(cnt_ref)

    @pl.when(ph == 0)
    def _():
        cnt_ref[...] += jnp.sum(oh0 + oh1, axis=1, keepdims=True)

    @pl.when((ph == 1) & (i == 0))
    def _():
        cnt = cnt_ref[...]
        padded = jnp.floor((cnt + (MOE_BLOCK - 1)) * (1.0 / MOE_BLOCK)) * MOE_BLOCK
        strict = _tri(MOE_EXPERTS, "strict").astype(F32)
        pstart = _dot_hi(strict, padded)
        pst_ref[...] = pstart
        run_ref[...] = jnp.zeros_like(run_ref)
        pend = (pstart + padded)[:, 0:1]
        blk0 = lax.broadcasted_iota(I32, (MOE_EXPERTS, nmeta), 1).astype(F32) * MOE_BLOCK
        be = jnp.sum((pend <= blk0).astype(F32), axis=0, keepdims=True)
        be = jnp.minimum(be, MOE_EXPERTS - 1.0)
        meta_ref[...] = jnp.broadcast_to(be, (8, nmeta)).astype(I32)

    @pl.when(ph == 1)
    def _():
        triu = _tri(tb, "upper").astype(BF16)
        cum0 = jnp.dot(oh0.astype(BF16), triu, preferred_element_type=F32)
        cum1 = jnp.dot(oh1.astype(BF16), triu, preferred_element_type=F32)
        tot0 = cum0[:, tb - 1:tb]
        tot1 = cum1[:, tb - 1:tb]
        base = run_ref[:, 0:1] + pst_ref[:, 0:1]
        d0 = jnp.sum(oh0 * (cum0 - 1.0 + base), axis=0, keepdims=True)
        d1 = jnp.sum(oh1 * (cum1 - 1.0 + base + tot0), axis=0, keepdims=True)
        run_ref[...] += tot0 + tot1
        sub = lax.broadcasted_iota(I32, (8, tb), 0)
        dest_ref[...] = jnp.where(sub == 0, d0, jnp.where(sub == 1, d1, 0.0)).astype(I32)


def _slots(route, n_blocks):
    n = route.shape[1]
    tb = min(256, n)
    nmeta = -(-n_blocks // LANES) * LANES
    return pl.pallas_call(
        functools.partial(_slots_kernel, tb=tb, nmeta=nmeta),
        grid=(2, n // tb),
        in_specs=[pl.BlockSpec((8, tb), lambda ph, i: (0, i))],
        out_specs=[pl.BlockSpec((8, tb), lambda ph, i: (0, i * ph)), pl.BlockSpec((8, nmeta), lambda ph, i: (0, 0))],
        out_shape=[jax.ShapeDtypeStruct((8, n), I32), jax.ShapeDtypeStruct((8, nmeta), I32)],
        scratch_shapes=[pltpu.VMEM((MOE_EXPERTS, LANES), F32), pltpu.VMEM((MOE_EXPERTS, LANES), F32),
                        pltpu.VMEM((MOE_EXPERTS, LANES), F32)],
        compiler_params=_cparams("arbitrary", "arbitrary"),
        name="moe_slots",
    )(route)


DMA_UNROLL = 8


def _tile_copy(src_ref, src_tok, dst_ref, dst_tok, sem):
    src = src_ref.at[pl.ds(pl.multiple_of(src_tok * TOK_ROWS, TOK_ROWS), TOK_ROWS), :]
    dst = dst_ref.at[pl.ds(pl.multiple_of(dst_tok * TOK_ROWS, TOK_ROWS), TOK_ROWS), :]
    return pltpu.make_async_copy(src, dst, sem)


def _dispatch_kernel(dest_ref, xt_ref, xs_in_ref, xs_ref, sem, *, tb):
    del xs_in_ref
    base = pl.program_id(0) * tb

    def start(r, _):
        _tile_copy(xt_ref, base + r, xs_ref, dest_ref[0, r], sem).start(priority=0)
        _tile_copy(xt_ref, base + r, xs_ref, dest_ref[1, r], sem).start(priority=1)
        return 0

    def wait(r, _):
        _tile_copy(xt_ref, base + r, xs_ref, dest_ref[0, r], sem).wait()
        _tile_copy(xt_ref, base + r, xs_ref, dest_ref[1, r], sem).wait()
        return 0

    lax.fori_loop(0, tb, start, 0, unroll=DMA_UNROLL)
    lax.fori_loop(0, tb, wait, 0, unroll=DMA_UNROLL)


def _dispatch(dest, x_tiles, total):
    n = x_tiles.shape[0] // TOK_ROWS
    tb = min(256, n)
    xs0 = jnp.zeros((total * TOK_ROWS, LANES), F32)
    return pl.pallas_call(
        functools.partial(_dispatch_kernel, tb=tb),
        grid=(n // tb,),
        in_specs=[pl.BlockSpec((8, tb), lambda i: (0, i), memory_space=pltpu.SMEM),
                  pl.BlockSpec(memory_space=pl.ANY),
                  pl.BlockSpec(memory_space=pl.ANY)],
        out_specs=pl.BlockSpec(memory_space=pl.ANY),
        out_shape=jax.ShapeDtypeStruct((total * TOK_ROWS, LANES), F32),
        scratch_shapes=[pltpu.SemaphoreType.DMA],
        input_output_aliases={2: 0},
        compiler_params=_cparams("arbitrary"),
        name="moe_dispatch",
    )(dest, x_tiles, xs0)


def _expert_kernel(be_ref, x_ref, wg_ref, wu_ref, wd_ref, o_ref):
    del be_ref
    x = _load_token_tiles(x_ref, MOE_BLOCK).astype(BF16)
    gate = jnp.dot(x, wg_ref[0], preferred_element_type=F32)
    up = jnp.dot(x, wu_ref[0], preferred_element_type=F32)
    hidden = (_silu(gate) * up).astype(BF16)
    _store_token_tiles(o_ref, jnp.dot(hidden, wd_ref[0], preferred_element_type=F32))


def _experts(blk_expert, xs, w_gate, w_up, w_down):
    rows = MOE_BLOCK * TOK_ROWS
    nb = xs.shape[0] // rows
    grid_spec = pltpu.PrefetchScalarGridSpec(
        num_scalar_prefetch=1,
        grid=(nb,),
        in_specs=[
            pl.BlockSpec((rows, LANES), lambda j, be: (j, 0)),
            pl.BlockSpec((1, D_MODEL, MOE_FF), lambda j, be: (be[j], 0, 0)),
            pl.BlockSpec((1, D_MODEL, MOE_FF), lambda j, be: (be[j], 0, 0)),
            pl.BlockSpec((1, MOE_FF, D_MODEL), lambda j, be: (be[j], 0, 0)),
        ],
        out_specs=pl.BlockSpec((rows, LANES), lambda j, be: (j, 0)),
    )
    return pl.pallas_call(
        _expert_kernel,
        grid_spec=grid_spec,
        out_shape=jax.ShapeDtypeStruct(xs.shape, F32),
        compiler_params=_cparams("arbitrary"),
        name="moe_experts",
    )(blk_expert, xs, w_gate, w_up, w_down)


def _combine_kernel(dest_ref, r_ref, x_ref, ys_ref, g_ref, b_ref, o_ref, buf_ref, sem, *, tb):
    def start(r, _):
        _tile_copy(ys_ref, dest_ref[0, r], buf_ref.at[0], r, sem).start(priority=0)
        _tile_copy(ys_ref, dest_ref[1, r], buf_ref.at[1], r, sem).start(priority=1)
        return 0

    def wait(r, _):
        _tile_copy(ys_ref, dest_ref[0, r], buf_ref.at[0], r, sem).wait()
        _tile_copy(ys_ref, dest_ref[1, r], buf_ref.at[1], r, sem).wait()
        return 0

    lax.fori_loop(0, tb, start, 0, unroll=DMA_UNROLL)
    eye = (lax.broadcasted_iota(I32, (8, LANES), 0) == lax.broadcasted_iota(I32, (8, LANES), 1)).astype(F32)
    w_cols = _dot_tn_hi(r_ref[...], eye)
    lax.fori_loop(0, tb, wait, 0, unroll=DMA_UNROLL)
    y = w_cols[:, 2:3] * _load_token_tiles(buf_ref.at[0], tb) + w_cols[:, 3:4] * _load_token_tiles(buf_ref.at[1], tb)
    o_ref[...] = _layer_norm(DEEPNORM_ALPHA * x_ref[...] + y, g_ref[...], b_ref[...])


def _combine(dest, route, x, ys, ln_g, ln_b):
    n = x.shape[0]
    tb = min(256, n)
    return pl.pallas_call(
        functools.partial(_combine_kernel, tb=tb),
        grid=(n // tb,),
        in_specs=[pl.BlockSpec((8, tb), lambda i: (0, i), memory_space=pltpu.SMEM),
                  pl.BlockSpec((8, tb), lambda i: (0, i)),
                  pl.BlockSpec((tb, D_MODEL), lambda i: (i, 0)),
                  pl.BlockSpec(memory_space=pl.ANY),
                  pl.BlockSpec((1, D_MODEL), lambda i: (0, 0)), pl.BlockSpec((1, D_MODEL), lambda i: (0, 0))],
        out_specs=pl.BlockSpec((tb, D_MODEL), lambda i: (i, 0)),
        out_shape=jax.ShapeDtypeStruct((n, D_MODEL), F32),
        scratch_shapes=[pltpu.VMEM((2, tb * TOK_ROWS, LANES), F32), pltpu.SemaphoreType.DMA],
        compiler_params=_cparams("arbitrary"),
        name="moe_combine",
    )(dest, route, x, ys, ln_g, ln_b)


def _lane_vec(vals, offset, width=LANES):
    return jnp.zeros((1, width), F32).at[0, offset:offset + vals.shape[0]].set(vals.astype(F32))


def _pack_w_in(w_in):
    big = jnp.concatenate([w_in[:, 0:1536], w_in[:, 1544:2056], w_in[:, 2056:3592], w_in[:, 3600:5136],
                           w_in[:, 5152:8224]], axis=1).astype(BF16)
    small = jnp.zeros((D_MODEL, LANES), F32)
    small = small.at[:, G_B:G_B + 4].set(w_in[:, 1536:1540])
    small = small.at[:, G_A:G_A + 4].set(w_in[:, 1540:1544])
    small = small.at[:, G_F:G_F + 8].set(w_in[:, 3592:3600])
    small = small.at[:, G_LR:G_LR + 16].set(w_in[:, 5136:5152])
    return big, small


def _layer(x, mem_flat, bsz, seq, w):
    n = bsz * seq
    w_big, w_small = _pack_w_in(w["w_in"])
    proj_big = _matmul(x, w_big, BF16, 1024, 1024)
    gates = _matmul(x, w_small, F32, 1024, LANES, hi=True)

    cum_row = _fox_cum(gates, _lane_vec(w["fox_f_bias"], G_F), bsz, seq)
    o_b = _fox_attention(proj_big, cum_row, bsz, seq)
    o_a = _gdn(proj_big, gates, w["gdn_conv_w"], _lane_vec(w["gdn_a_log"], G_A), _lane_vec(w["gdn_dt_bias"], G_A),
               w["gdn_norm_w"].reshape(1, GDN_DV), bsz, seq)
    w2_pad = jnp.zeros((LANES, GLA_QK), F32).at[G_LR:G_LR + GLA_RANK].set(w["gla_w_gate2"])
    o_c = _gla(proj_big, gates, w2_pad, w["gla_b_gate"].reshape(1, GLA_QK), w["gla_norm_w"].reshape(1, GLA_DV),
               bsz, seq)
    x = _merge(o_a, o_b, o_c, proj_big, x, w["p_gdn"].astype(BF16), w["p_fox"].astype(BF16),
               w["p_gla"].astype(BF16), w["b_merge"].reshape(N_BRANCH, D_MODEL), w["w_out"].astype(BF16),
               w["ln1_g"].reshape(1, D_MODEL), w["ln1_b"].reshape(1, D_MODEL))

    mem_len = mem_flat.shape[0] // bsz
    kv = _matmul(mem_flat, w["xa_wkv"].astype(BF16), BF16, 512, 1024).reshape(bsz, mem_len, 2 * D_MODEL)
    x, x_tiles = _xattn(x, kv, w["xa_wq"].astype(BF16), w["xa_wo"].astype(BF16), w["ln2_g"].reshape(1, D_MODEL),
                        w["ln2_b"].reshape(1, D_MODEL), bsz, seq)

    w_rt = jnp.zeros((R_ROWS, D_MODEL), F32).at[0:MOE_GROUPS].set(w["moe_w_group"].T).at[8:].set(w["moe_w_expert"].T)
    b_rt = jnp.zeros((R_ROWS, 1), F32).at[0:MOE_GROUPS, 0].set(w["moe_b_group"]).at[8:, 0].set(w["moe_b_expert"])
    route = _router(x, w_rt, b_rt)
    nk = 2 * n
    total = -(-nk // MOE_BLOCK) * MOE_BLOCK + MOE_EXPERTS * MOE_BLOCK
    n_blocks = total // MOE_BLOCK
    dest, meta = _slots(route, n_blocks)
    xs = _dispatch(dest, x_tiles, total)
    ys = _experts(meta[0, :n_blocks], xs, w["moe_w_gate"].astype(BF16), w["moe_w_up"].astype(BF16),
                  w["moe_w_down"].astype(BF16))
    return _combine(dest, route, x, ys, w["ln3_g"].reshape(1, D_MODEL), w["ln3_b"].reshape(1, D_MODEL))


_PARAM_NAMES = ("w_in", "gdn_conv_w", "gdn_a_log", "gdn_dt_bias", "gdn_norm_w", "fox_f_bias", "gla_w_gate2",
                "gla_b_gate", "gla_norm_w", "p_gdn", "p_fox", "p_gla", "b_merge", "w_out", "ln1_g", "ln1_b",
                "xa_wq", "xa_wkv", "xa_wo", "ln2_g", "ln2_b", "moe_w_group", "moe_b_group", "moe_w_expert",
                "moe_b_expert", "moe_w_gate", "moe_w_up", "moe_w_down", "ln3_g", "ln3_b")


def kernel(x, mem, w_in, gdn_conv_w, gdn_a_log, gdn_dt_bias, gdn_norm_w, fox_f_bias, gla_w_gate2, gla_b_gate, gla_norm_w, p_gdn, p_fox, p_gla, b_merge, w_out, ln1_g, ln1_b, xa_wq, xa_wkv, xa_wo, ln2_g, ln2_b, moe_w_group, moe_b_group, moe_w_expert, moe_b_expert, moe_w_gate, moe_w_up, moe_w_down, ln3_g, ln3_b):
    params = (w_in, gdn_conv_w, gdn_a_log, gdn_dt_bias, gdn_norm_w, fox_f_bias, gla_w_gate2, gla_b_gate, gla_norm_w,
              p_gdn, p_fox, p_gla, b_merge, w_out, ln1_g, ln1_b, xa_wq, xa_wkv, xa_wo, ln2_g, ln2_b, moe_w_group,
              moe_b_group, moe_w_expert, moe_b_expert, moe_w_gate, moe_w_up, moe_w_down, ln3_g, ln3_b)
    bsz, seq, d = x.shape
    h = x.reshape(bsz * seq, d)
    mem_flat = mem.reshape(bsz * mem.shape[1], d)
    for l in range(w_in.shape[0]):
        h = _layer(h, mem_flat, bsz, seq, {name: p[l] for name, p in zip(_PARAM_NAMES, params)})
    return h.reshape(bsz, seq, d)
```

```python
import functools

import jax
import jax.numpy as jnp
from jax import lax
from jax.experimental import pallas as pl
from jax.experimental.pallas import tpu as pltpu

F32 = jnp.float32
BF16 = jnp.bfloat16
I32 = jnp.int32
HIGHEST = lax.Precision.HIGHEST

D_MODEL = 1024
DEPTH = 2
GDN_HEADS = 4
GDN_DK = 128
GDN_DV = 128
GDN_QK = GDN_HEADS * GDN_DK
GDN_V = GDN_HEADS * GDN_DV
GDN_CONV = 4
CHUNK = 64
FOX_HEADS = 8
FOX_DH = 64
FOX_W = FOX_HEADS * FOX_DH
GLA_HEADS = 4
GLA_DK = 64
GLA_DV = 128
GLA_QK = GLA_HEADS * GLA_DK
GLA_V = GLA_HEADS * GLA_DV
GLA_RANK = 16
GLA_TAU = 16.0
N_BRANCH = 3
XA_HEADS = 4
XA_DH = D_MODEL // XA_HEADS
MOE_GROUPS = 4
MOE_PER_GROUP = 8
MOE_EXPERTS = MOE_GROUPS * MOE_PER_GROUP
MOE_FF = D_MODEL // 4
MOE_BLOCK = 256
DEEPNORM_ALPHA = (2 * DEPTH) ** 0.25
LN_EPS = 1e-5
RMS_EPS = 1e-6

LANES = 128
CONV_PAD = 16
NEG_BIG = -1e30
VMEM_LIMIT = 56 * 1024 * 1024

C_GDN_QKV = 0
C_GDN_Z = 1536
C_FOX_Q = 2048
C_FOX_K = 2560
C_FOX_V = 3072
C_GLA_Q = 3584
C_GLA_K = 3840
C_GLA_V = 4096
C_GLA_R = 4608
C_MERGE = 5120
N_BIG = 8192
G_B, G_A, G_F, G_LR = 0, 4, 8, 16


def _cparams(*sem):
    return pltpu.CompilerParams(dimension_semantics=sem, vmem_limit_bytes=VMEM_LIMIT)


def _sigmoid(x):
    return 1.0 / (1.0 + jnp.exp(-x))


def _silu(x):
    return x * _sigmoid(x)


def _softplus(x):
    return jnp.maximum(x, 0.0) + jnp.log(1.0 + jnp.exp(-jnp.abs(x)))


def _log_sigmoid(x):
    return jnp.minimum(x, 0.0) - jnp.log(1.0 + jnp.exp(-jnp.abs(x)))


def _layer_norm(y, g, b):
    mu = jnp.mean(y, axis=-1, keepdims=True)
    yc = y - mu
    var = jnp.mean(yc * yc, axis=-1, keepdims=True)
    return yc * lax.rsqrt(var + LN_EPS) * g + b


def _dot(a, b):
    return jnp.dot(a.astype(BF16), b.astype(BF16), preferred_element_type=F32)


def _dot_nt(a, b):
    return lax.dot_general(a.astype(BF16), b.astype(BF16), (((1,), (1,)), ((), ())),
                           preferred_element_type=F32)


def _dot_tn(a, b):
    return lax.dot_general(a.astype(BF16), b.astype(BF16), (((0,), (0,)), ((), ())),
                           preferred_element_type=F32)


def _dot_hi(a, b):
    return jnp.dot(a, b, precision=HIGHEST, preferred_element_type=F32)


def _dot_tn_hi(a, b):
    return lax.dot_general(a, b, (((0,), (0,)), ((), ())), precision=HIGHEST, preferred_element_type=F32)


def _tri(n, kind):
    r = lax.broadcasted_iota(I32, (n, n), 0)
    c = lax.broadcasted_iota(I32, (n, n), 1)
    if kind == "lower":
        return r >= c
    if kind == "strict":
        return r > c
    return r <= c


def _chunk_masks(rows, c):
    r = lax.broadcasted_iota(I32, (rows, rows), 0)
    col = lax.broadcasted_iota(I32, (rows, rows), 1)
    same = (r | (c - 1)) == (col | (c - 1))
    causal = same & (r >= col)
    return causal, same & (r > col), causal.astype(BF16), (same & (r <= col)).astype(BF16)


def _split2(x):
    hi = x.astype(BF16)
    return hi, (x - hi.astype(F32)).astype(BF16)


def _dot_01(m01, x):
    hi, lo = _split2(x)
    return jnp.dot(m01, hi, preferred_element_type=F32) + jnp.dot(m01, lo, preferred_element_type=F32)


def _dot_tn_01(x, m01):
    hi, lo = _split2(x)
    dims = (((0,), (0,)), ((), ()))
    return (lax.dot_general(hi, m01, dims, preferred_element_type=F32)
            + lax.dot_general(lo, m01, dims, preferred_element_type=F32))


def _chunk_last_rows(x, c):
    rows = x.shape[0]
    r = lax.broadcasted_iota(I32, (rows, 1), 0)
    out = jnp.broadcast_to(x[c - 1:c, :], x.shape)
    for j in range(1, rows // c):
        out = jnp.where(r >= j * c, x[(j + 1) * c - 1:(j + 1) * c, :], out)
    return out


def _mm_kernel(a_ref, b_ref, o_ref):
    o_ref[...] = jnp.dot(a_ref[...].astype(BF16), b_ref[...], preferred_element_type=F32).astype(o_ref.dtype)


def _mm_hi_kernel(a_ref, b_ref, o_ref):
    o_ref[...] = _dot_hi(a_ref[...].astype(F32), b_ref[...]).astype(o_ref.dtype)


def _matmul(a, b, out_dtype, tm, tn, hi=False):
    m, k = a.shape
    n = b.shape[1]
    tm, tn = min(tm, m), min(tn, n)
    return pl.pallas_call(
        _mm_hi_kernel if hi else _mm_kernel,
        grid=(m // tm, n // tn),
        in_specs=[pl.BlockSpec((tm, k), lambda i, j: (i, 0)), pl.BlockSpec((k, tn), lambda i, j: (0, j))],
        out_specs=pl.BlockSpec((tm, tn), lambda i, j: (i, j)),
        out_shape=jax.ShapeDtypeStruct((m, n), out_dtype),
        compiler_params=_cparams("parallel", "parallel"),
        name="proj_hi" if hi else "proj",
    )(a, b)


def _fcum_kernel(g_ref, fb_ref, row_ref, *, seq, blk):
    triu = _tri(blk, "upper").astype(F32)
    carry = jnp.zeros((LANES, 1), F32)
    for i in range(seq // blk):
        lf = _log_sigmoid(g_ref[i * blk:(i + 1) * blk, :] + fb_ref[...])
        cum_t = _dot_tn_hi(lf, triu) + carry
        row_ref[0, :, i * blk:(i + 1) * blk] = cum_t[G_F:G_F + FOX_HEADS, :]
        carry = cum_t[:, blk - 1:blk]


def _fox_cum(gates, fb_vec, bsz, seq):
    blk = min(256, seq)
    return pl.pallas_call(
        functools.partial(_fcum_kernel, seq=seq, blk=blk),
        grid=(bsz,),
        in_specs=[pl.BlockSpec((seq, LANES), lambda b: (b, 0)), pl.BlockSpec((1, LANES), lambda b: (0, 0))],
        out_specs=pl.BlockSpec((1, FOX_HEADS, seq), lambda b: (b, 0, 0)),
        out_shape=jax.ShapeDtypeStruct((bsz, FOX_HEADS, seq), F32),
        compiler_params=_cparams("parallel"),
        name="fox_cum",
    )(gates, fb_vec)


FOX_GROUP = 4


def _fox_kernel(q_ref, k_ref, v_ref, cr_ref, o_ref, va_ref, *, tq, seq):
    grp = pl.program_id(1)
    qi = pl.program_id(2)
    lane = lax.broadcasted_iota(I32, (tq, LANES), 1)
    diag = lax.broadcasted_iota(I32, (tq, tq), 1) <= lax.broadcasted_iota(I32, (tq, tq), 0)
    scale = jnp.asarray(FOX_DH ** -0.5, BF16)

    def head_lanes(g, rows):
        ln = lax.broadcasted_iota(I32, (rows, LANES), 1)
        return (ln >= FOX_DH * (g % 2)) & (ln < FOX_DH * (g % 2 + 1))

    @pl.when(qi == 0)
    def _():
        for g in range(FOX_GROUP):
            vp = v_ref[:, (g // 2) * LANES:(g // 2 + 1) * LANES]
            va_ref[g] = jnp.where(head_lanes(g, seq), vp, jnp.ones_like(vp))

    qs = []
    for g in range(FOX_GROUP):
        qp = q_ref[:, (g // 2) * LANES:(g // 2 + 1) * LANES]
        qs.append(jnp.where(head_lanes(g, tq), qp, jnp.zeros_like(qp)) * scale)

    def tile(j, carry, masked):
        start = pl.multiple_of(j * tq, tq)
        out = []
        for g in range(FOX_GROUP):
            m, acc = carry[g]
            ps = slice((g // 2) * LANES, (g // 2 + 1) * LANES)
            c_k = cr_ref[0, pl.ds(FOX_GROUP * grp + g, 1), pl.ds(start, tq)]
            s = _dot_nt(qs[g], k_ref[pl.ds(start, tq), ps]) - c_k
            if masked:
                s = jnp.where(diag, s, NEG_BIG)
            m_new = jnp.maximum(m, jnp.max(s, axis=1, keepdims=True))
            pr = jnp.exp(s - m_new).astype(BF16)
            acc = jnp.exp(m - m_new) * acc + jnp.dot(pr, va_ref[g, pl.ds(start, tq), :], preferred_element_type=F32)
            out.append((m_new, acc))
        return tuple(out)

    init = tuple((jnp.full((tq, 1), NEG_BIG, F32), jnp.zeros((tq, LANES), F32)) for _ in range(FOX_GROUP))
    carry = lax.fori_loop(0, qi, lambda j, cy: tile(j, cy, False), init)
    carry = tile(qi, carry, True)
    res = [acc / pltpu.roll(acc, FOX_DH, axis=1) for _, acc in carry]
    for p in range(FOX_GROUP // 2):
        o_ref[:, p * LANES:(p + 1) * LANES] = jnp.where(lane < FOX_DH, res[2 * p], res[2 * p + 1]).astype(o_ref.dtype)


def _fox_attention(proj_big, cum_row, bsz, seq):
    tq = min(256, seq)
    nq = seq // tq
    n = bsz * seq
    gw = FOX_GROUP * FOX_DH
    qb, kb, vb = C_FOX_Q // gw, C_FOX_K // gw, C_FOX_V // gw
    return pl.pallas_call(
        functools.partial(_fox_kernel, tq=tq, seq=seq),
        grid=(bsz, FOX_HEADS // FOX_GROUP, nq),
        in_specs=[
            pl.BlockSpec((tq, gw), lambda b, p, i: (b * nq + i, qb + p)),
            pl.BlockSpec((seq, gw), lambda b, p, i: (b, kb + p)),
            pl.BlockSpec((seq, gw), lambda b, p, i: (b, vb + p)),
            pl.BlockSpec((1, FOX_HEADS, seq), lambda b, p, i: (b, 0, 0)),
        ],
        out_specs=pl.BlockSpec((tq, gw), lambda b, p, i: (b * nq + i, p)),
        out_shape=jax.ShapeDtypeStruct((n, FOX_W), BF16),
        scratch_shapes=[pltpu.VMEM((FOX_GROUP, seq, LANES), BF16)],
        compiler_params=_cparams("parallel", "parallel", "arbitrary"),
        name="fox_attn",
    )(proj_big, proj_big, proj_big, cum_row)


def _unit_lower_inverse(low, nil):
    n = low.shape[0]
    eye = (lax.broadcasted_iota(I32, (n, n), 0) == lax.broadcasted_iota(I32, (n, n), 1)).astype(F32)
    inv = eye - low
    power = _dot(low, low)
    span = 2
    while span < nil:
        inv = inv + _dot(inv, power)
        span *= 2
        if span < nil:
            power = _dot(power, power)
    return inv


GDN_GROUP = 4


def _gdn_kernel(qkv_ref, z_ref, g_ref, cw_ref, nega_ref, dt_ref, nw_ref, o_ref,
                pad_ref, u_ref, wq_ref, kd_ref, at_ref, cd_ref, st_ref, mk_ref, tm_ref, *, seq):
    c = CHUNK
    n_chunks = seq // c
    gr_rows = GDN_GROUP * c
    pad_ref[0:CONV_PAD, :] = jnp.zeros((CONV_PAD, 3 * GDN_QK), BF16)
    pad_ref[CONV_PAD:CONV_PAD + seq, :] = qkv_ref[...]
    st_ref[...] = jnp.zeros_like(st_ref)
    @pl.when(pl.program_id(0) == 0)
    def _():
        causal, strict, tril, triu = _chunk_masks(gr_rows, c)
        mk_ref[0] = causal.astype(F32)
        mk_ref[1] = strict.astype(F32)
        tm_ref[0] = tril
        tm_ref[1] = triu

    cw = cw_ref[...]
    neg_a = -jnp.exp(nega_ref[...])
    lane1 = lax.broadcasted_iota(I32, (1, LANES), 1)
    neg_a = jnp.where((lane1 >= G_A) & (lane1 < G_A + GDN_HEADS), neg_a, 0.0)
    dt = dt_ref[...]
    nw = nw_ref[...]
    sub8 = lax.broadcasted_iota(I32, (8, LANES), 0)

    def local_group(gi, _):
        r0 = pl.multiple_of(gi * gr_rows, gr_rows)
        win = pad_ref[pl.ds(r0, gr_rows + CONV_PAD), :]
        gs = g_ref[pl.ds(r0, gr_rows), :]
        beta_all = _sigmoid(gs)
        g_all = neg_a * _softplus(gs + dt)
        cum = _dot_01(tm_ref[0], g_all)
        cum_t = cum.T
        cum_last = _chunk_last_rows(cum, c)
        e_last = jnp.exp(cum_last)

        def conv(col0):
            cols = slice(col0, col0 + LANES)
            xw = win[:, cols].astype(F32)
            acc = xw[CONV_PAD - 3:CONV_PAD - 3 + gr_rows] * cw[0:1, cols]
            for i in range(1, GDN_CONV):
                acc = acc + xw[CONV_PAD - 3 + i:CONV_PAD - 3 + i + gr_rows] * cw[i:i + 1, cols]
            return _silu(acc)

        def l2n(x):
            return x * lax.rsqrt(jnp.sum(x * x, axis=-1, keepdims=True) + RMS_EPS)

        cd_tiles = [jnp.zeros((8, LANES), F32) for _ in range(GDN_GROUP)]
        for h in range(GDN_HEADS):
            hs = slice(h * GDN_DV, (h + 1) * GDN_DV)
            q = l2n(conv(h * GDN_DK)) * (GDN_DK ** -0.5)
            k = l2n(conv(GDN_QK + h * GDN_DK))
            v = conv(2 * GDN_QK + h * GDN_DV)
            beta = beta_all[:, G_B + h:G_B + h + 1]
            gc = cum[:, G_A + h:G_A + h + 1]
            gr = cum_t[G_A + h:G_A + h + 1, :]
            gl = cum_last[:, G_A + h:G_A + h + 1]
            eg = jnp.exp(gc)
            decay = jnp.exp(jnp.minimum(gc - gr, 0.0))
            kb = k * beta
            low = _dot_nt(kb, k) * (decay * mk_ref[1])
            t_inv = _unit_lower_inverse(low, c)
            uw = _dot(t_inv, jnp.concatenate([v * beta, kb * eg], axis=1))
            attn = (_dot_nt(q, k) * (decay * mk_ref[0])).astype(BF16)
            qd = (q * eg).astype(BF16)
            u_ref[pl.ds(r0, gr_rows), hs] = uw[:, 0:GDN_DV]
            kd_ref[pl.ds(r0, gr_rows), hs] = (k * jnp.exp(gl - gc)).astype(BF16)
            w = uw[:, GDN_DV:2 * GDN_DV].astype(BF16)
            for j in range(GDN_GROUP):
                js = slice(j * c, (j + 1) * c)
                wq_ref[pl.ds(2 * r0 + 2 * j * c, c), hs] = w[js]
                wq_ref[pl.ds(2 * r0 + 2 * j * c + c, c), hs] = qd[js]
                at_ref[h, pl.ds(r0 + j * c, c), :] = attn[js, js]
                cd_tiles[j] = jnp.where(sub8 == h, e_last[j * c:j * c + 1, G_A + h:G_A + h + 1], cd_tiles[j])
        for j in range(GDN_GROUP):
            cd_ref[pl.ds(pl.multiple_of(gi * (8 * GDN_GROUP), 8 * GDN_GROUP) + 8 * j, 8), :] = cd_tiles[j]
        return 0

    def recurrent_chunk(ci, _):
        r0 = pl.multiple_of(ci * c, c)
        z = z_ref[pl.ds(r0, c), :].astype(F32)
        cd_tile = cd_ref[pl.ds(pl.multiple_of(ci * 8, 8), 8), :]
        outs = []
        for h in range(GDN_HEADS):
            hs = slice(h * GDN_DV, (h + 1) * GDN_DV)
            state = st_ref[h]
            ws = jnp.dot(wq_ref[pl.ds(pl.multiple_of(2 * r0, 2 * c), 2 * c), hs], state.astype(BF16),
                         preferred_element_type=F32)
            v_new = (u_ref[pl.ds(r0, c), hs] - ws[0:c]).astype(BF16)
            out = ws[c:2 * c] + jnp.dot(at_ref[h, pl.ds(r0, c), :], v_new, preferred_element_type=F32)
            st_ref[h] = state * cd_tile[h:h + 1, :] + _dot_tn(kd_ref[pl.ds(r0, c), hs], v_new)
            rms = out * lax.rsqrt(jnp.mean(out * out, axis=-1, keepdims=True) + RMS_EPS) * nw
            outs.append(rms * _silu(z[:, hs]))
        o_ref[pl.ds(r0, c), :] = jnp.concatenate(outs, axis=1).astype(o_ref.dtype)
        return 0

    lax.fori_loop(0, n_chunks // GDN_GROUP, local_group, 0)
    lax.fori_loop(0, n_chunks, recurrent_chunk, 0)


def _gdn(proj_big, gates, conv_w, a_vec, dt_vec, norm_w, bsz, seq):
    n = bsz * seq
    return pl.pallas_call(
        functools.partial(_gdn_kernel, seq=seq),
        grid=(bsz,),
        in_specs=[
            pl.BlockSpec((seq, 3 * GDN_QK), lambda b: (b, 0)),
            pl.BlockSpec((seq, GDN_V), lambda b: (b, C_GDN_Z // GDN_V)),
            pl.BlockSpec((seq, LANES), lambda b: (b, 0)),
            pl.BlockSpec((GDN_CONV, 3 * GDN_QK), lambda b: (0, 0)),
            pl.BlockSpec((1, LANES), lambda b: (0, 0)),
            pl.BlockSpec((1, LANES), lambda b: (0, 0)),
            pl.BlockSpec((1, GDN_DV), lambda b: (0, 0)),
        ],
        out_specs=pl.BlockSpec((seq, GDN_V), lambda b: (b, 0)),
        out_shape=jax.ShapeDtypeStruct((n, GDN_V), BF16),
        scratch_shapes=[pltpu.VMEM((seq + CONV_PAD, 3 * GDN_QK), BF16),
                        pltpu.VMEM((seq, GDN_V), F32),
                        pltpu.VMEM((2 * seq, GDN_QK), BF16),
                        pltpu.VMEM((seq, GDN_QK), BF16),
                        pltpu.VMEM((GDN_HEADS, seq, CHUNK), BF16),
                        pltpu.VMEM((seq // CHUNK * 8, LANES), F32),
                        pltpu.VMEM((GDN_HEADS, GDN_DK, GDN_DV), F32),
                        pltpu.VMEM((2, GDN_GROUP * CHUNK, GDN_GROUP * CHUNK), F32),
                        pltpu.VMEM((2, GDN_GROUP * CHUNK, GDN_GROUP * CHUNK), BF16)],
        compiler_params=_cparams("arbitrary"),
        name="gdn",
    )(proj_big, proj_big, gates, conv_w, a_vec, dt_vec, norm_w)


def _gla_kernel(q_ref, k_ref, v_ref, r_ref, g_ref, w2_ref, bg_ref, nw_ref, o_ref, st_ref, mk_ref, tm_ref, *, seq):
    c = CHUNK
    rows = GDN_GROUP * c
    st_ref[...] = jnp.zeros_like(st_ref)
    @pl.when(pl.program_id(0) == 0)
    def _():
        causal, _, tril, _ = _chunk_masks(rows, c)
        mk_ref[...] = causal.astype(F32)
        tm_ref[...] = tril

    w2 = w2_ref[...]
    bg = bg_ref[...]
    nw = nw_ref[...]
    lane1 = lax.broadcasted_iota(I32, (1, LANES), 1)

    def group_body(gi, _):
        r0 = pl.multiple_of(gi * rows, rows)
        gs = g_ref[pl.ds(r0, rows), :]
        log_a = _log_sigmoid(_dot_hi(gs, w2) + bg) * (1.0 / GLA_TAU)
        cum = _dot_01(tm_ref[...], log_a)
        cum_last = _chunk_last_rows(cum, c)
        q = q_ref[pl.ds(r0, rows), :].astype(F32)
        k = k_ref[pl.ds(r0, rows), :].astype(F32)
        v = v_ref[pl.ds(r0, rows), :]
        r = r_ref[pl.ds(r0, rows), :].astype(F32)
        q_t = q * jnp.exp(cum) * (GLA_DK ** -0.5)
        k_t = k * jnp.exp(-cum)
        k_dec = k * jnp.exp(cum_last - cum)
        cdec = jnp.exp(cum_last)
        outs = []
        for h in range(GLA_HEADS):
            p, hh = divmod(h, 2)
            sl = slice(p * LANES, (p + 1) * LANES)
            head_lanes = (lane1 >= GLA_DK * hh) & (lane1 < GLA_DK * (hh + 1))
            qh = jnp.where(head_lanes, q_t[:, sl], 0.0).astype(BF16)
            kdh = jnp.where(head_lanes, k_dec[:, sl], 0.0).astype(BF16)
            attn = jnp.where(mk_ref[...] > 0.5, _dot_nt(qh, k_t[:, sl]), 0.0)
            vh = v[:, h * GLA_DV:(h + 1) * GLA_DV]
            intra = _dot(attn, vh)
            state_t = st_ref[h]
            inter = []
            for j in range(GDN_GROUP):
                js = slice(j * c, (j + 1) * c)
                inter.append(_dot_nt(qh[js], state_t))
                state_t = state_t * cdec[j * c:j * c + 1, sl] + _dot_tn(vh[js], kdh[js])
            st_ref[h] = state_t
            out = intra + jnp.concatenate(inter, axis=0)
            rms = out * lax.rsqrt(jnp.mean(out * out, axis=-1, keepdims=True) + RMS_EPS) * nw
            outs.append(rms * _silu(r[:, h * GLA_DV:(h + 1) * GLA_DV]))
        o_ref[pl.ds(r0, rows), :] = jnp.concatenate(outs, axis=1).astype(o_ref.dtype)
        return 0

    lax.fori_loop(0, seq // rows, group_body, 0)


def _gla(proj_big, gates, w2_pad, b_gate, norm_w, bsz, seq):
    n = bsz * seq
    return pl.pallas_call(
        functools.partial(_gla_kernel, seq=seq),
        grid=(bsz,),
        in_specs=[
            pl.BlockSpec((seq, GLA_QK), lambda b: (b, C_GLA_Q // GLA_QK)),
            pl.BlockSpec((seq, GLA_QK), lambda b: (b, C_GLA_K // GLA_QK)),
            pl.BlockSpec((seq, GLA_V), lambda b: (b, C_GLA_V // GLA_V)),
            pl.BlockSpec((seq, GLA_V), lambda b: (b, C_GLA_R // GLA_V)),
            pl.BlockSpec((seq, LANES), lambda b: (b, 0)),
            pl.BlockSpec((LANES, GLA_QK), lambda b: (0, 0)),
            pl.BlockSpec((1, GLA_QK), lambda b: (0, 0)),
            pl.BlockSpec((1, GLA_DV), lambda b: (0, 0)),
        ],
        out_specs=pl.BlockSpec((seq, GLA_V), lambda b: (b, 0)),
        out_shape=jax.ShapeDtypeStruct((n, GLA_V), BF16),
        scratch_shapes=[pltpu.VMEM((GLA_HEADS, GLA_DV, LANES), F32),
                        pltpu.VMEM((GDN_GROUP * CHUNK, GDN_GROUP * CHUNK), F32),
                        pltpu.VMEM((GDN_GROUP * CHUNK, GDN_GROUP * CHUNK), BF16)],
        compiler_params=_cparams("arbitrary"),
        name="gla",
    )(proj_big, proj_big, proj_big, proj_big, gates, w2_pad, b_gate, norm_w)


def _merge_kernel(oa_ref, ob_ref, oc_ref, m0_ref, m1_ref, m2_ref, x_ref, pa_ref, pb_ref, pc_ref, bm_ref,
                  wo_ref, g_ref, b_ref, o_ref):
    bm = bm_ref[...]
    merged = _sigmoid(m0_ref[...].astype(F32) + bm[0:1]) * jnp.dot(oa_ref[...], pa_ref[...],
                                                                   preferred_element_type=F32)
    merged += _sigmoid(m1_ref[...].astype(F32) + bm[1:2]) * jnp.dot(ob_ref[...], pb_ref[...],
                                                                    preferred_element_type=F32)
    merged += _sigmoid(m2_ref[...].astype(F32) + bm[2:3]) * jnp.dot(oc_ref[...], pc_ref[...],
                                                                    preferred_element_type=F32)
    mix = jnp.dot(merged.astype(BF16), wo_ref[...], preferred_element_type=F32)
    o_ref[...] = _layer_norm(DEEPNORM_ALPHA * x_ref[...] + mix, g_ref[...], b_ref[...])


def _merge(o_a, o_b, o_c, proj_big, x, p_a, p_b, p_c, b_merge, w_out, ln_g, ln_b):
    n = x.shape[0]
    tm = min(512, n)
    mb = C_MERGE // D_MODEL
    row = lambda i: (i, 0)
    full = lambda i: (0, 0)
    return pl.pallas_call(
        _merge_kernel,
        grid=(n // tm,),
        in_specs=[
            pl.BlockSpec((tm, GDN_V), row), pl.BlockSpec((tm, FOX_W), row), pl.BlockSpec((tm, GLA_V), row),
            pl.BlockSpec((tm, D_MODEL), lambda i: (i, mb)),
            pl.BlockSpec((tm, D_MODEL), lambda i: (i, mb + 1)),
            pl.BlockSpec((tm, D_MODEL), lambda i: (i, mb + 2)),
            pl.BlockSpec((tm, D_MODEL), row),
            pl.BlockSpec((GDN_V, D_MODEL), full), pl.BlockSpec((FOX_W, D_MODEL), full),
            pl.BlockSpec((GLA_V, D_MODEL), full),
            pl.BlockSpec((N_BRANCH, D_MODEL), full),
            pl.BlockSpec((D_MODEL, D_MODEL), full),
            pl.BlockSpec((1, D_MODEL), full), pl.BlockSpec((1, D_MODEL), full),
        ],
        out_specs=pl.BlockSpec((tm, D_MODEL), row),
        out_shape=jax.ShapeDtypeStruct((n, D_MODEL), F32),
        compiler_params=_cparams("parallel"),
        name="merge_out",
    )(o_a, o_b, o_c, proj_big, proj_big, proj_big, x, p_a, p_b, p_c, b_merge, w_out, ln_g, ln_b)


TOK_ROWS = D_MODEL // LANES


def _store_token_tiles(t_ref, y):
    m = y.shape[0]
    for s in range(TOK_ROWS):
        t_ref[pl.ds(s, m, stride=TOK_ROWS), :] = y[:, s * LANES:(s + 1) * LANES]


def _load_token_tiles(t_ref, m):
    return jnp.concatenate([t_ref[pl.ds(s, m, stride=TOK_ROWS), :] for s in range(TOK_ROWS)], axis=1)


def _xattn_kernel(x_ref, kv_ref, wq_ref, wo_ref, g_ref, b_ref, o_ref, t_ref):
    x = x_ref[...]
    q = jnp.dot(x.astype(BF16), wq_ref[...], preferred_element_type=F32).astype(BF16)
    kv = kv_ref[0]
    xa = jnp.zeros(x.shape, F32)
    for h in range(XA_HEADS):
        sl = slice(h * XA_DH, (h + 1) * XA_DH)
        s = _dot_nt(q[:, sl], kv[:, sl]) * (XA_DH ** -0.5)
        s = s - jnp.max(s, axis=-1, keepdims=True)
        e = jnp.exp(s)
        pr = e / jnp.sum(e, axis=-1, keepdims=True)
        o_h = _dot(pr, kv[:, D_MODEL + h * XA_DH:D_MODEL + (h + 1) * XA_DH])
        xa = xa + jnp.dot(o_h.astype(BF16), wo_ref[sl, :], preferred_element_type=F32)
    y = _layer_norm(DEEPNORM_ALPHA * x + xa, g_ref[...], b_ref[...])
    o_ref[...] = y
    _store_token_tiles(t_ref, y)


def _xattn(x, kv, wq, wo, ln_g, ln_b, bsz, seq):
    n = x.shape[0]
    tm = min(512, seq)
    nt = seq // tm
    mem = kv.shape[1]
    full = lambda b, i: (0, 0)
    return pl.pallas_call(
        _xattn_kernel,
        grid=(bsz, nt),
        in_specs=[
            pl.BlockSpec((tm, D_MODEL), lambda b, i: (b * nt + i, 0)),
            pl.BlockSpec((1, mem, 2 * D_MODEL), lambda b, i: (b, 0, 0)),
            pl.BlockSpec((D_MODEL, D_MODEL), full), pl.BlockSpec((D_MODEL, D_MODEL), full),
            pl.BlockSpec((1, D_MODEL), full), pl.BlockSpec((1, D_MODEL), full),
        ],
        out_specs=[pl.BlockSpec((tm, D_MODEL), lambda b, i: (b * nt + i, 0)),
                   pl.BlockSpec((tm * TOK_ROWS, LANES), lambda b, i: (b * nt + i, 0))],
        out_shape=[jax.ShapeDtypeStruct((n, D_MODEL), F32), jax.ShapeDtypeStruct((n * TOK_ROWS, LANES), F32)],
        compiler_params=_cparams("parallel", "parallel"),
        name="xattn",
    )(x, kv, wq, wo, ln_g, ln_b)


R_ROWS = 8 + MOE_EXPERTS


def _router_kernel(x_ref, w_ref, b_ref, o_ref):
    tm = x_ref.shape[0]
    logits = lax.dot_general(w_ref[...], x_ref[...], (((1,), (1,)), ((), ())), precision=HIGHEST,
                             preferred_element_type=F32) + b_ref[...]
    sub = lax.broadcasted_iota(I32, (8, tm), 0)
    grp = jnp.where(sub < MOE_GROUPS, logits[0:8], NEG_BIG)
    gmax = jnp.max(grp, axis=0, keepdims=True)
    p_top = 1.0 / jnp.sum(jnp.exp(grp - gmax), axis=0, keepdims=True)
    g_sel = jnp.min(jnp.where(grp == gmax, sub, 8), axis=0, keepdims=True)
    sel = logits[8:16]
    for g in range(1, MOE_GROUPS):
        sel = jnp.where(g_sel == g, logits[8 + 8 * g:16 + 8 * g], sel)
    e = jnp.exp(sel - jnp.max(sel, axis=0, keepdims=True))
    p_in = e / jnp.sum(e, axis=0, keepdims=True)
    v1 = jnp.max(p_in, axis=0, keepdims=True)
    i1 = jnp.min(jnp.where(p_in == v1, sub, 8), axis=0, keepdims=True)
    rest = jnp.where(sub == i1, -1.0, p_in)
    v2 = jnp.max(rest, axis=0, keepdims=True)
    i2 = jnp.min(jnp.where(rest == v2, sub, 8), axis=0, keepdims=True)
    tot = v1 + v2
    base = g_sel * MOE_PER_GROUP
    rows = [(base + i1).astype(F32), (base + i2).astype(F32), v1 / tot * p_top, v2 / tot * p_top]
    out = jnp.zeros((8, tm), F32)
    for r, val in enumerate(rows):
        out = jnp.where(sub == r, val, out)
    o_ref[...] = out


def _router(x, w_rt, b_rt):
    n = x.shape[0]
    tm = min(512, n)
    return pl.pallas_call(
        _router_kernel,
        grid=(n // tm,),
        in_specs=[pl.BlockSpec((tm, D_MODEL), lambda i: (i, 0)),
                  pl.BlockSpec((R_ROWS, D_MODEL), lambda i: (0, 0)),
                  pl.BlockSpec((R_ROWS, 1), lambda i: (0, 0))],
        out_specs=pl.BlockSpec((8, tm), lambda i: (0, i)),
        out_shape=jax.ShapeDtypeStruct((8, n), F32),
        compiler_params=_cparams("parallel"),
        name="router",
    )(x, w_rt, b_rt)


def _slots_kernel(r_ref, dest_ref, meta_ref, cnt_ref, run_ref, pst_ref, *, tb, nmeta):
    ph = pl.program_id(0)
    i = pl.program_id(1)
    ex = lax.broadcasted_iota(I32, (MOE_EXPERTS, tb), 0).astype(F32)
    oh0 = (ex == r_ref[0:1, :]).astype(F32)
    oh1 = (ex == r_ref[1:2, :]).astype(F32)

    @pl.when((ph == 0) & (i == 0))
    def _():
        cnt_ref[...] = jnp.zeros_like(cnt_ref)

    @pl.when(ph == 0)
    def _():
        cnt_ref[...] += jnp.sum(oh0 + oh1, axis=1, keepdims=True)

    @pl.when((ph == 1) & (i == 0))
    def _():
        cnt = cnt_ref[...]
        padded = jnp.floor((cnt + (MOE_BLOCK - 1)) * (1.0 / MOE_BLOCK)) * MOE_BLOCK
        strict = _tri(MOE_EXPERTS, "strict").astype(F32)
        pstart = _dot_hi(strict, padded)
        pst_ref[...] = pstart
        run_ref[...] = jnp.zeros_like(run_ref)
        pend = (pstart + padded)[:, 0:1]
        blk0 = lax.broadcasted_iota(I32, (MOE_EXPERTS, nmeta), 1).astype(F32) * MOE_BLOCK
        be = jnp.sum((pend <= blk0).astype(F32), axis=0, keepdims=True)
        be = jnp.minimum(be, MOE_EXPERTS - 1.0)
        meta_ref[...] = jnp.broadcast_to(be, (8, nmeta)).astype(I32)

    @pl.when(ph == 1)
    def _():
        triu = _tri(tb, "upper").astype(BF16)
        cum0 = jnp.dot(oh0.astype(BF16), triu, preferred_element_type=F32)
        cum1 = jnp.dot(oh1.astype(BF16), triu, preferred_element_type=F32)
        tot0 = cum0[:, tb - 1:tb]
        tot1 = cum1[:, tb - 1:tb]
        base = run_ref[:, 0:1] + pst_ref[:, 0:1]
        d0 = jnp.sum(oh0 * (cum0 - 1.0 + base), axis=0, keepdims=True)
        d1 = jnp.sum(oh1 * (cum1 - 1.0 + base + tot0), axis=0, keepdims=True)
        run_ref[...] += tot0 + tot1
        sub = lax.broadcasted_iota(I32, (8, tb), 0)
        dest_ref[...] = jnp.where(sub == 0, d0, jnp.where(sub == 1, d1, 0.0)).astype(I32)


def _slots(route, n_blocks):
    n = route.shape[1]
    tb = min(256, n)
    nmeta = -(-n_blocks // LANES) * LANES
    return pl.pallas_call(
        functools.partial(_slots_kernel, tb=tb, nmeta=nmeta),
        grid=(2, n // tb),
        in_specs=[pl.BlockSpec((8, tb), lambda ph, i: (0, i))],
        out_specs=[pl.BlockSpec((8, tb), lambda ph, i: (0, i * ph)), pl.BlockSpec((8, nmeta), lambda ph, i: (0, 0))],
        out_shape=[jax.ShapeDtypeStruct((8, n), I32), jax.ShapeDtypeStruct((8, nmeta), I32)],
        scratch_shapes=[pltpu.VMEM((MOE_EXPERTS, LANES), F32), pltpu.VMEM((MOE_EXPERTS, LANES), F32),
                        pltpu.VMEM((MOE_EXPERTS, LANES), F32)],
        compiler_params=_cparams("arbitrary", "arbitrary"),
        name="moe_slots",
    )(route)


DMA_UNROLL = 8


def _tile_copy(src_ref, src_tok, dst_ref, dst_tok, sem):
    src = src_ref.at[pl.ds(pl.multiple_of(src_tok * TOK_ROWS, TOK_ROWS), TOK_ROWS), :]
    dst = dst_ref.at[pl.ds(pl.multiple_of(dst_tok * TOK_ROWS, TOK_ROWS), TOK_ROWS), :]
    return pltpu.make_async_copy(src, dst, sem)


def _dispatch_kernel(dest_ref, xt_ref, xs_in_ref, xs_ref, sem, *, tb):
    del xs_in_ref

    def start(r, _):
        _tile_copy(xt_ref, r, xs_ref, dest_ref[0, r], sem).start(priority=0)
        _tile_copy(xt_ref, r, xs_ref, dest_ref[1, r], sem).start(priority=1)
        return 0

    def wait(r, _):
        _tile_copy(xt_ref, r, xs_ref, dest_ref[0, r], sem).wait()
        _tile_copy(xt_ref, r, xs_ref, dest_ref[1, r], sem).wait()
        return 0

    lax.fori_loop(0, tb, start, 0, unroll=DMA_UNROLL)
    lax.fori_loop(0, tb, wait, 0, unroll=DMA_UNROLL)


def _dispatch(dest, x_tiles, total):
    n = x_tiles.shape[0] // TOK_ROWS
    tb = min(256, n)
    xs0 = jnp.zeros((total * TOK_ROWS, LANES), F32)
    return pl.pallas_call(
        functools.partial(_dispatch_kernel, tb=tb),
        grid=(n // tb,),
        in_specs=[pl.BlockSpec((8, tb), lambda i: (0, i), memory_space=pltpu.SMEM),
                  pl.BlockSpec((tb * TOK_ROWS, LANES), lambda i: (i, 0)),
                  pl.BlockSpec(memory_space=pl.ANY)],
        out_specs=pl.BlockSpec(memory_space=pl.ANY),
        out_shape=jax.ShapeDtypeStruct((total * TOK_ROWS, LANES), F32),
        scratch_shapes=[pltpu.SemaphoreType.DMA],
        input_output_aliases={2: 0},
        compiler_params=_cparams("arbitrary"),
        name="moe_dispatch",
    )(dest, x_tiles, xs0)


def _expert_kernel(be_ref, x_ref, wg_ref, wu_ref, wd_ref, o_ref):
    del be_ref
    x = _load_token_tiles(x_ref, MOE_BLOCK).astype(BF16)
    gate = jnp.dot(x, wg_ref[0], preferred_element_type=F32)
    up = jnp.dot(x, wu_ref[0], preferred_element_type=F32)
    hidden = (_silu(gate) * up).astype(BF16)
    _store_token_tiles(o_ref, jnp.dot(hidden, wd_ref[0], preferred_element_type=F32))


def _experts(blk_expert, xs, w_gate, w_up, w_down):
    rows = MOE_BLOCK * TOK_ROWS
    nb = xs.shape[0] // rows
    grid_spec = pltpu.PrefetchScalarGridSpec(
        num_scalar_prefetch=1,
        grid=(nb,),
        in_specs=[
            pl.BlockSpec((rows, LANES), lambda j, be: (j, 0)),
            pl.BlockSpec((1, D_MODEL, MOE_FF), lambda j, be: (be[j], 0, 0)),
            pl.BlockSpec((1, D_MODEL, MOE_FF), lambda j, be: (be[j], 0, 0)),
            pl.BlockSpec((1, MOE_FF, D_MODEL), lambda j, be: (be[j], 0, 0)),
        ],
        out_specs=pl.BlockSpec((rows, LANES), lambda j, be: (j, 0)),
    )
    return pl.pallas_call(
        _expert_kernel,
        grid_spec=grid_spec,
        out_shape=jax.ShapeDtypeStruct(xs.shape, F32),
        compiler_params=_cparams("arbitrary"),
        name="moe_experts",
    )(blk_expert, xs, w_gate, w_up, w_down)


def _combine_kernel(dest_ref, r_ref, x_ref, ys_ref, g_ref, b_ref, o_ref, buf_ref, sem, *, tb):
    def start(r, _):
        _tile_copy(ys_ref, dest_ref[0, r], buf_ref.at[0], r, sem).start(priority=0)
        _tile_copy(ys_ref, dest_ref[1, r], buf_ref.at[1], r, sem).start(priority=1)
        return 0

    def wait(r, _):
        _tile_copy(ys_ref, dest_ref[0, r], buf_ref.at[0], r, sem).wait()
        _tile_copy(ys_ref, dest_ref[1, r], buf_ref.at[1], r, sem).wait()
        return 0

    lax.fori_loop(0, tb, start, 0, unroll=DMA_UNROLL)
    eye = (lax.broadcasted_iota(I32, (8, LANES), 0) == lax.broadcasted_iota(I32, (8, LANES), 1)).astype(F32)
    w_cols = _dot_tn_hi(r_ref[...], eye)
    lax.fori_loop(0, tb, wait, 0, unroll=DMA_UNROLL)
    y = w_cols[:, 2:3] * _load_token_tiles(buf_ref.at[0], tb) + w_cols[:, 3:4] * _load_token_tiles(buf_ref.at[1], tb)
    o_ref[...] = _layer_norm(DEEPNORM_ALPHA * x_ref[...] + y, g_ref[...], b_ref[...])


def _combine(dest, route, x, ys, ln_g, ln_b):
    n = x.shape[0]
    tb = min(256, n)
    return pl.pallas_call(
        functools.partial(_combine_kernel, tb=tb),
        grid=(n // tb,),
        in_specs=[pl.BlockSpec((8, tb), lambda i: (0, i), memory_space=pltpu.SMEM),
                  pl.BlockSpec((8, tb), lambda i: (0, i)),
                  pl.BlockSpec((tb, D_MODEL), lambda i: (i, 0)),
                  pl.BlockSpec(memory_space=pl.ANY),
                  pl.BlockSpec((1, D_MODEL), lambda i: (0, 0)), pl.BlockSpec((1, D_MODEL), lambda i: (0, 0))],
        out_specs=pl.BlockSpec((tb, D_MODEL), lambda i: (i, 0)),
        out_shape=jax.ShapeDtypeStruct((n, D_MODEL), F32),
        scratch_shapes=[pltpu.VMEM((2, tb * TOK_ROWS, LANES), F32), pltpu.SemaphoreType.DMA],
        compiler_params=_cparams("arbitrary"),
        name="moe_combine",
    )(dest, route, x, ys, ln_g, ln_b)


def _lane_vec(vals, offset, width=LANES):
    return jnp.zeros((1, width), F32).at[0, offset:offset + vals.shape[0]].set(vals.astype(F32))


def _pack_w_in(w_in):
    big = jnp.concatenate([w_in[:, 0:1536], w_in[:, 1544:2056], w_in[:, 2056:3592], w_in[:, 3600:5136],
                           w_in[:, 5152:8224]], axis=1).astype(BF16)
    small = jnp.zeros((D_MODEL, LANES), F32)
    small = small.at[:, G_B:G_B + 4].set(w_in[:, 1536:1540])
    small = small.at[:, G_A:G_A + 4].set(w_in[:, 1540:1544])
    small = small.at[:, G_F:G_F + 8].set(w_in[:, 3592:3600])
    small = small.at[:, G_LR:G_LR + 16].set(w_in[:, 5136:5152])
    return big, small


def _layer(x, mem_flat, bsz, seq, w):
    n = bsz * seq
    w_big, w_small = _pack_w_in(w["w_in"])
    proj_big = _matmul(x, w_big, BF16, 1024, 1024)
    gates = _matmul(x, w_small, F32, 1024, LANES, hi=True)

    cum_row = _fox_cum(gates, _lane_vec(w["fox_f_bias"], G_F), bsz, seq)
    o_b = _fox_attention(proj_big, cum_row, bsz, seq)
    o_a = _gdn(proj_big, gates, w["gdn_conv_w"], _lane_vec(w["gdn_a_log"], G_A), _lane_vec(w["gdn_dt_bias"], G_A),
               w["gdn_norm_w"].reshape(1, GDN_DV), bsz, seq)
    w2_pad = jnp.zeros((LANES, GLA_QK), F32).at[G_LR:G_LR + GLA_RANK].set(w["gla_w_gate2"])
    o_c = _gla(proj_big, gates, w2_pad, w["gla_b_gate"].reshape(1, GLA_QK), w["gla_norm_w"].reshape(1, GLA_DV),
               bsz, seq)
    x = _merge(o_a, o_b, o_c, proj_big, x, w["p_gdn"].astype(BF16), w["p_fox"].astype(BF16),
               w["p_gla"].astype(BF16), w["b_merge"].reshape(N_BRANCH, D_MODEL), w["w_out"].astype(BF16),
               w["ln1_g"].reshape(1, D_MODEL), w["ln1_b"].reshape(1, D_MODEL))

    mem_len = mem_flat.shape[0] // bsz
    kv = _matmul(mem_flat, w["xa_wkv"].astype(BF16), BF16, 512, 1024).reshape(bsz, mem_len, 2 * D_MODEL)
    x, x_tiles = _xattn(x, kv, w["xa_wq"].astype(BF16), w["xa_wo"].astype(BF16), w["ln2_g"].reshape(1, D_MODEL),
                        w["ln2_b"].reshape(1, D_MODEL), bsz, seq)

    w_rt = jnp.zeros((R_ROWS, D_MODEL), F32).at[0:MOE_GROUPS].set(w["moe_w_group"].T).at[8:].set(w["moe_w_expert"].T)
    b_rt = jnp.zeros((R_ROWS, 1), F32).at[0:MOE_GROUPS, 0].set(w["moe_b_group"]).at[8:, 0].set(w["moe_b_expert"])
    route = _router(x, w_rt, b_rt)
    nk = 2 * n
    total = -(-nk // MOE_BLOCK) * MOE_BLOCK + MOE_EXPERTS * MOE_BLOCK
    n_blocks = total // MOE_BLOCK
    dest, meta = _slots(route, n_blocks)
    xs = _dispatch(dest, x_tiles, total)
    ys = _experts(meta[0, :n_blocks], xs, w["moe_w_gate"].astype(BF16), w["moe_w_up"].astype(BF16),
                  w["moe_w_down"].astype(BF16))
    return _combine(dest, route, x, ys, w["ln3_g"].reshape(1, D_MODEL), w["ln3_b"].reshape(1, D_MODEL))


_PARAM_NAMES = ("w_in", "gdn_conv_w", "gdn_a_log", "gdn_dt_bias", "gdn_norm_w", "fox_f_bias", "gla_w_gate2",
                "gla_b_gate", "gla_norm_w", "p_gdn", "p_fox", "p_gla", "b_merge", "w_out", "ln1_g", "ln1_b",
                "xa_wq", "xa_wkv", "xa_wo", "ln2_g", "ln2_b", "moe_w_group", "moe_b_group", "moe_w_expert",
                "moe_b_expert", "moe_w_gate", "moe_w_up", "moe_w_down", "ln3_g", "ln3_b")


def kernel(x, mem, w_in, gdn_conv_w, gdn_a_log, gdn_dt_bias, gdn_norm_w, fox_f_bias, gla_w_gate2, gla_b_gate, gla_norm_w, p_gdn, p_fox, p_gla, b_merge, w_out, ln1_g, ln1_b, xa_wq, xa_wkv, xa_wo, ln2_g, ln2_b, moe_w_group, moe_b_group, moe_w_expert, moe_b_expert, moe_w_gate, moe_w_up, moe_w_down, ln3_g, ln3_b):
    params = (w_in, gdn_conv_w, gdn_a_log, gdn_dt_bias, gdn_norm_w, fox_f_bias, gla_w_gate2, gla_b_gate, gla_norm_w,
              p_gdn, p_fox, p_gla, b_merge, w_out, ln1_g, ln1_b, xa_wq, xa_wkv, xa_wo, ln2_g, ln2_b, moe_w_group,
              moe_b_group, moe_w_expert, moe_b_expert, moe_w_gate, moe_w_up, moe_w_down, ln3_g, ln3_b)
    bsz, seq, d = x.shape
    h = x.reshape(bsz * seq, d)
    mem_flat = mem.reshape(bsz * mem.shape[1], d)
    for l in range(w_in.shape[0]):
        h = _layer(h, mem_flat, bsz, seq, {name: p[l] for name, p in zip(_PARAM_NAMES, params)})
    return h.reshape(bsz, seq, d)
```

```python
import functools

import jax
import jax.numpy as jnp
from jax import lax
from jax.experimental import pallas as pl
from jax.experimental.pallas import tpu as pltpu

F32 = jnp.float32
BF16 = jnp.bfloat16
I32 = jnp.int32
HIGHEST = lax.Precision.HIGHEST

D_MODEL = 1024
DEPTH = 2
GDN_HEADS = 4
GDN_DK = 128
GDN_DV = 128
GDN_QK = GDN_HEADS * GDN_DK
GDN_V = GDN_HEADS * GDN_DV
GDN_CONV = 4
CHUNK = 64
FOX_HEADS = 8
FOX_DH = 64
FOX_W = FOX_HEADS * FOX_DH
GLA_HEADS = 4
GLA_DK = 64
GLA_DV = 128
GLA_QK = GLA_HEADS * GLA_DK
GLA_V = GLA_HEADS * GLA_DV
GLA_RANK = 16
GLA_TAU = 16.0
N_BRANCH = 3
XA_HEADS = 4
XA_DH = D_MODEL // XA_HEADS
MOE_GROUPS = 4
MOE_PER_GROUP = 8
MOE_EXPERTS = MOE_GROUPS * MOE_PER_GROUP
MOE_FF = D_MODEL // 4
MOE_BLOCK = 256
DEEPNORM_ALPHA = (2 * DEPTH) ** 0.25
LN_EPS = 1e-5
RMS_EPS = 1e-6

LANES = 128
CONV_PAD = 16
NEG_BIG = -1e30
VMEM_LIMIT = 56 * 1024 * 1024

C_GDN_QKV = 0
C_GDN_Z = 1536
C_FOX_Q = 2048
C_FOX_K = 2560
C_FOX_V = 3072
C_GLA_Q = 3584
C_GLA_K = 3840
C_GLA_V = 4096
C_GLA_R = 4608
C_MERGE = 5120
N_BIG = 8192
G_B, G_A, G_F, G_LR = 0, 4, 8, 16


def _cparams(*sem):
    return pltpu.CompilerParams(dimension_semantics=sem, vmem_limit_bytes=VMEM_LIMIT)


def _sigmoid(x):
    return 1.0 / (1.0 + jnp.exp(-x))


def _silu(x):
    return x * _sigmoid(x)


def _softplus(x):
    return jnp.maximum(x, 0.0) + jnp.log(1.0 + jnp.exp(-jnp.abs(x)))


def _log_sigmoid(x):
    return jnp.minimum(x, 0.0) - jnp.log(1.0 + jnp.exp(-jnp.abs(x)))


def _layer_norm(y, g, b):
    mu = jnp.mean(y, axis=-1, keepdims=True)
    yc = y - mu
    var = jnp.mean(yc * yc, axis=-1, keepdims=True)
    return yc * lax.rsqrt(var + LN_EPS) * g + b


def _dot(a, b):
    return jnp.dot(a.astype(BF16), b.astype(BF16), preferred_element_type=F32)


def _dot_nt(a, b):
    return lax.dot_general(a.astype(BF16), b.astype(BF16), (((1,), (1,)), ((), ())),
                           preferred_element_type=F32)


def _dot_tn(a, b):
    return lax.dot_general(a.astype(BF16), b.astype(BF16), (((0,), (0,)), ((), ())),
                           preferred_element_type=F32)


def _dot_hi(a, b):
    return jnp.dot(a, b, precision=HIGHEST, preferred_element_type=F32)


def _dot_tn_hi(a, b):
    return lax.dot_general(a, b, (((0,), (0,)), ((), ())), precision=HIGHEST, preferred_element_type=F32)


def _tri(n, kind):
    r = lax.broadcasted_iota(I32, (n, n), 0)
    c = lax.broadcasted_iota(I32, (n, n), 1)
    if kind == "lower":
        return r >= c
    if kind == "strict":
        return r > c
    return r <= c


def _chunk_masks(rows, c):
    r = lax.broadcasted_iota(I32, (rows, rows), 0)
    col = lax.broadcasted_iota(I32, (rows, rows), 1)
    same = (r | (c - 1)) == (col | (c - 1))
    causal = same & (r >= col)
    return causal, same & (r > col), causal.astype(BF16), (same & (r <= col)).astype(BF16)


def _split2(x):
    hi = x.astype(BF16)
    return hi, (x - hi.astype(F32)).astype(BF16)


def _dot_01(m01, x):
    hi, lo = _split2(x)
    return jnp.dot(m01, hi, preferred_element_type=F32) + jnp.dot(m01, lo, preferred_element_type=F32)


def _dot_tn_01(x, m01):
    hi, lo = _split2(x)
    dims = (((0,), (0,)), ((), ()))
    return (lax.dot_general(hi, m01, dims, preferred_element_type=F32)
            + lax.dot_general(lo, m01, dims, preferred_element_type=F32))


def _chunk_last_rows(x, c):
    rows = x.shape[0]
    r = lax.broadcasted_iota(I32, (rows, 1), 0)
    out = jnp.broadcast_to(x[c - 1:c, :], x.shape)
    for j in range(1, rows // c):
        out = jnp.where(r >= j * c, x[(j + 1) * c - 1:(j + 1) * c, :], out)
    return out


def _mm_kernel(a_ref, b_ref, o_ref):
    o_ref[...] = jnp.dot(a_ref[...].astype(BF16), b_ref[...], preferred_element_type=F32).astype(o_ref.dtype)


def _mm_hi_kernel(a_ref, b_ref, o_ref):
    o_ref[...] = _dot_hi(a_ref[...].astype(F32), b_ref[...]).astype(o_ref.dtype)


def _matmul(a, b, out_dtype, tm, tn, hi=False):
    m, k = a.shape
    n = b.shape[1]
    tm, tn = min(tm, m), min(tn, n)
    return pl.pallas_call(
        _mm_hi_kernel if hi else _mm_kernel,
        grid=(m // tm, n // tn),
        in_specs=[pl.BlockSpec((tm, k), lambda i, j: (i, 0)), pl.BlockSpec((k, tn), lambda i, j: (0, j))],
        out_specs=pl.BlockSpec((tm, tn), lambda i, j: (i, j)),
        out_shape=jax.ShapeDtypeStruct((m, n), out_dtype),
        compiler_params=_cparams("parallel", "parallel"),
        name="proj_hi" if hi else "proj",
    )(a, b)


def _fcum_kernel(g_ref, fb_ref, row_ref, *, seq, blk):
    triu = _tri(blk, "upper").astype(F32)
    carry = jnp.zeros((LANES, 1), F32)
    for i in range(seq // blk):
        lf = _log_sigmoid(g_ref[i * blk:(i + 1) * blk, :] + fb_ref[...])
        cum_t = _dot_tn_hi(lf, triu) + carry
        row_ref[0, :, i * blk:(i + 1) * blk] = cum_t[G_F:G_F + FOX_HEADS, :]
        carry = cum_t[:, blk - 1:blk]


def _fox_cum(gates, fb_vec, bsz, seq):
    blk = min(256, seq)
    return pl.pallas_call(
        functools.partial(_fcum_kernel, seq=seq, blk=blk),
        grid=(bsz,),
        in_specs=[pl.BlockSpec((seq, LANES), lambda b: (b, 0)), pl.BlockSpec((1, LANES), lambda b: (0, 0))],
        out_specs=pl.BlockSpec((1, FOX_HEADS, seq), lambda b: (b, 0, 0)),
        out_shape=jax.ShapeDtypeStruct((bsz, FOX_HEADS, seq), F32),
        compiler_params=_cparams("parallel"),
        name="fox_cum",
    )(gates, fb_vec)


FOX_GROUP = 8


def _fox_kernel(q_ref, k_ref, v_ref, cr_ref, o_ref, va_ref, *, tq, seq):
    grp = pl.program_id(1)
    qi = pl.program_id(2)
    lane = lax.broadcasted_iota(I32, (tq, LANES), 1)
    diag = lax.broadcasted_iota(I32, (tq, tq), 1) <= lax.broadcasted_iota(I32, (tq, tq), 0)
    scale = jnp.asarray(FOX_DH ** -0.5, BF16)

    def head_lanes(g, rows):
        ln = lax.broadcasted_iota(I32, (rows, LANES), 1)
        return (ln >= FOX_DH * (g % 2)) & (ln < FOX_DH * (g % 2 + 1))

    @pl.when(qi == 0)
    def _():
        for g in range(FOX_GROUP):
            vp = v_ref[:, (g // 2) * LANES:(g // 2 + 1) * LANES]
            va_ref[g] = jnp.where(head_lanes(g, seq), vp, jnp.ones_like(vp))

    qs = []
    for g in range(FOX_GROUP):
        qp = q_ref[:, (g // 2) * LANES:(g // 2 + 1) * LANES]
        qs.append(jnp.where(head_lanes(g, tq), qp, jnp.zeros_like(qp)) * scale)

    def tile(j, carry, masked):
        start = pl.multiple_of(j * tq, tq)
        out = []
        for g in range(FOX_GROUP):
            m, acc = carry[g]
            ps = slice((g // 2) * LANES, (g // 2 + 1) * LANES)
            c_k = cr_ref[0, pl.ds(FOX_GROUP * grp + g, 1), pl.ds(start, tq)]
            s = _dot_nt(qs[g], k_ref[pl.ds(start, tq), ps]) - c_k
            if masked:
                s = jnp.where(diag, s, NEG_BIG)
            m_new = jnp.maximum(m, jnp.max(s, axis=1, keepdims=True))
            pr = jnp.exp(s - m_new).astype(BF16)
            acc = jnp.exp(m - m_new) * acc + jnp.dot(pr, va_ref[g, pl.ds(start, tq), :], preferred_element_type=F32)
            out.append((m_new, acc))
        return tuple(out)

    init = tuple((jnp.full((tq, 1), NEG_BIG, F32), jnp.zeros((tq, LANES), F32)) for _ in range(FOX_GROUP))
    carry = lax.fori_loop(0, qi, lambda j, cy: tile(j, cy, False), init)
    carry = tile(qi, carry, True)
    res = [acc / pltpu.roll(acc, FOX_DH, axis=1) for _, acc in carry]
    for p in range(FOX_GROUP // 2):
        o_ref[:, p * LANES:(p + 1) * LANES] = jnp.where(lane < FOX_DH, res[2 * p], res[2 * p + 1]).astype(o_ref.dtype)


def _fox_attention(proj_big, cum_row, bsz, seq):
    tq = min(256, seq)
    nq = seq // tq
    n = bsz * seq
    gw = FOX_GROUP * FOX_DH
    qb, kb, vb = C_FOX_Q // gw, C_FOX_K // gw, C_FOX_V // gw
    return pl.pallas_call(
        functools.partial(_fox_kernel, tq=tq, seq=seq),
        grid=(bsz, FOX_HEADS // FOX_GROUP, nq),
        in_specs=[
            pl.BlockSpec((tq, gw), lambda b, p, i: (b * nq + i, qb + p)),
            pl.BlockSpec((seq, gw), lambda b, p, i: (b, kb + p)),
            pl.BlockSpec((seq, gw), lambda b, p, i: (b, vb + p)),
            pl.BlockSpec((1, FOX_HEADS, seq), lambda b, p, i: (b, 0, 0)),
        ],
        out_specs=pl.BlockSpec((tq, gw), lambda b, p, i: (b * nq + i, p)),
        out_shape=jax.ShapeDtypeStruct((n, FOX_W), BF16),
        scratch_shapes=[pltpu.VMEM((FOX_GROUP, seq, LANES), BF16)],
        compiler_params=_cparams("parallel", "parallel", "arbitrary"),
        name="fox_attn",
    )(proj_big, proj_big, proj_big, cum_row)


def _unit_lower_inverse(low, nil):
    n = low.shape[0]
    eye = (lax.broadcasted_iota(I32, (n, n), 0) == lax.broadcasted_iota(I32, (n, n), 1)).astype(F32)
    inv = eye - low
    power = _dot(low, low)
    span = 2
    while span < nil:
        inv = inv + _dot(inv, power)
        span *= 2
        if span < nil:
            power = _dot(power, power)
    return inv


GDN_GROUP = 4


def _gdn_kernel(qkv_ref, z_ref, g_ref, cw_ref, nega_ref, dt_ref, nw_ref, o_ref,
                pad_ref, u_ref, wq_ref, kd_ref, at_ref, cd_ref, st_ref, mk_ref, tm_ref, *, seq):
    c = CHUNK
    n_chunks = seq // c
    gr_rows = GDN_GROUP * c
    pad_ref[0:CONV_PAD, :] = jnp.zeros((CONV_PAD, 3 * GDN_QK), BF16)
    pad_ref[CONV_PAD:CONV_PAD + seq, :] = qkv_ref[...]
    st_ref[...] = jnp.zeros_like(st_ref)
    @pl.when(pl.program_id(0) == 0)
    def _():
        causal, strict, tril, triu = _chunk_masks(gr_rows, c)
        mk_ref[0] = causal.astype(F32)
        mk_ref[1] = strict.astype(F32)
        tm_ref[0] = tril
        tm_ref[1] = triu

    cw = cw_ref[...]
    neg_a = -jnp.exp(nega_ref[...])
    lane1 = lax.broadcasted_iota(I32, (1, LANES), 1)
    neg_a = jnp.where((lane1 >= G_A) & (lane1 < G_A + GDN_HEADS), neg_a, 0.0)
    dt = dt_ref[...]
    nw = nw_ref[...]
    sub8 = lax.broadcasted_iota(I32, (8, LANES), 0)

    def local_group(gi, _):
        r0 = pl.multiple_of(gi * gr_rows, gr_rows)
        win = pad_ref[pl.ds(r0, gr_rows + CONV_PAD), :]
        gs = g_ref[pl.ds(r0, gr_rows), :]
        beta_all = _sigmoid(gs)
        g_all = neg_a * _softplus(gs + dt)
        cum = _dot_01(tm_ref[0], g_all)
        cum_t = cum.T
        cum_last = _chunk_last_rows(cum, c)
        e_last = jnp.exp(cum_last)

        def conv(col0):
            cols = slice(col0, col0 + LANES)
            xw = win[:, cols].astype(F32)
            acc = xw[CONV_PAD - 3:CONV_PAD - 3 + gr_rows] * cw[0:1, cols]
            for i in range(1, GDN_CONV):
                acc = acc + xw[CONV_PAD - 3 + i:CONV_PAD - 3 + i + gr_rows] * cw[i:i + 1, cols]
            return _silu(acc)

        def l2n(x):
            return x * lax.rsqrt(jnp.sum(x * x, axis=-1, keepdims=True) + RMS_EPS)

        cd_tiles = [jnp.zeros((8, LANES), F32) for _ in range(GDN_GROUP)]
        for h in range(GDN_HEADS):
            hs = slice(h * GDN_DV, (h + 1) * GDN_DV)
            q = l2n(conv(h * GDN_DK)) * (GDN_DK ** -0.5)
            k = l2n(conv(GDN_QK + h * GDN_DK))
            v = conv(2 * GDN_QK + h * GDN_DV)
            beta = beta_all[:, G_B + h:G_B + h + 1]
            gc = cum[:, G_A + h:G_A + h + 1]
            gr = cum_t[G_A + h:G_A + h + 1, :]
            gl = cum_last[:, G_A + h:G_A + h + 1]
            eg = jnp.exp(gc)
            decay = jnp.exp(jnp.minimum(gc - gr, 0.0))
            kb = k * beta
            low = _dot_nt(kb, k) * (decay * mk_ref[1])
            t_inv = _unit_lower_inverse(low, c)
            uw = _dot(t_inv, jnp.concatenate([v * beta, kb * eg], axis=1))
            attn = (_dot_nt(q, k) * (decay * mk_ref[0])).astype(BF16)
            qd = (q * eg).astype(BF16)
            u_ref[pl.ds(r0, gr_rows), hs] = uw[:, 0:GDN_DV]
            kd_ref[pl.ds(r0, gr_rows), hs] = (k * jnp.exp(gl - gc)).astype(BF16)
            w = uw[:, GDN_DV:2 * GDN_DV].astype(BF16)
            for j in range(GDN_GROUP):
                js = slice(j * c, (j + 1) * c)
                wq_ref[pl.ds(2 * r0 + 2 * j * c, c), hs] = w[js]
                wq_ref[pl.ds(2 * r0 + 2 * j * c + c, c), hs] = qd[js]
                at_ref[h, pl.ds(r0 + j * c, c), :] = attn[js, js]
                cd_tiles[j] = jnp.where(sub8 == h, e_last[j * c:j * c + 1, G_A + h:G_A + h + 1], cd_tiles[j])
        for j in range(GDN_GROUP):
            cd_ref[pl.ds(pl.multiple_of(gi * (8 * GDN_GROUP), 8 * GDN_GROUP) + 8 * j, 8), :] = cd_tiles[j]
        return 0

    def recurrent_chunk(ci, _):
        r0 = pl.multiple_of(ci * c, c)
        z = z_ref[pl.ds(r0, c), :].astype(F32)
        cd_tile = cd_ref[pl.ds(pl.multiple_of(ci * 8, 8), 8), :]
        outs = []
        for h in range(GDN_HEADS):
            hs = slice(h * GDN_DV, (h + 1) * GDN_DV)
            state = st_ref[h]
            ws = jnp.dot(wq_ref[pl.ds(pl.multiple_of(2 * r0, 2 * c), 2 * c), hs], state.astype(BF16),
                         preferred_element_type=F32)
            v_new = (u_ref[pl.ds(r0, c), hs] - ws[0:c]).astype(BF16)
            out = ws[c:2 * c] + jnp.dot(at_ref[h, pl.ds(r0, c), :], v_new, preferred_element_type=F32)
            st_ref[h] = state * cd_tile[h:h + 1, :] + _dot_tn(kd_ref[pl.ds(r0, c), hs], v_new)
            rms = out * lax.rsqrt(jnp.mean(out * out, axis=-1, keepdims=True) + RMS_EPS) * nw
            outs.append(rms * _silu(z[:, hs]))
        o_ref[pl.ds(r0, c), :] = jnp.concatenate(outs, axis=1).astype(o_ref.dtype)
        return 0

    lax.fori_loop(0, n_chunks // GDN_GROUP, local_group, 0)
    lax.fori_loop(0, n_chunks, recurrent_chunk, 0)


def _gdn(proj_big, gates, conv_w, a_vec, dt_vec, norm_w, bsz, seq):
    n = bsz * seq
    return pl.pallas_call(
        functools.partial(_gdn_kernel, seq=seq),
        grid=(bsz,),
        in_specs=[
            pl.BlockSpec((seq, 3 * GDN_QK), lambda b: (b, 0)),
            pl.BlockSpec((seq, GDN_V), lambda b: (b, C_GDN_Z // GDN_V)),
            pl.BlockSpec((seq, LANES), lambda b: (b, 0)),
            pl.BlockSpec((GDN_CONV, 3 * GDN_QK), lambda b: (0, 0)),
            pl.BlockSpec((1, LANES), lambda b: (0, 0)),
            pl.BlockSpec((1, LANES), lambda b: (0, 0)),
            pl.BlockSpec((1, GDN_DV), lambda b: (0, 0)),
        ],
        out_specs=pl.BlockSpec((seq, GDN_V), lambda b: (b, 0)),
        out_shape=jax.ShapeDtypeStruct((n, GDN_V), BF16),
        scratch_shapes=[pltpu.VMEM((seq + CONV_PAD, 3 * GDN_QK), BF16),
                        pltpu.VMEM((seq, GDN_V), F32),
                        pltpu.VMEM((2 * seq, GDN_QK), BF16),
                        pltpu.VMEM((seq, GDN_QK), BF16),
                        pltpu.VMEM((GDN_HEADS, seq, CHUNK), BF16),
                        pltpu.VMEM((seq // CHUNK * 8, LANES), F32),
                        pltpu.VMEM((GDN_HEADS, GDN_DK, GDN_DV), F32),
                        pltpu.VMEM((2, GDN_GROUP * CHUNK, GDN_GROUP * CHUNK), F32),
                        pltpu.VMEM((2, GDN_GROUP * CHUNK, GDN_GROUP * CHUNK), BF16)],
        compiler_params=_cparams("arbitrary"),
        name="gdn",
    )(proj_big, proj_big, gates, conv_w, a_vec, dt_vec, norm_w)


def _gla_kernel(q_ref, k_ref, v_ref, r_ref, g_ref, w2_ref, bg_ref, nw_ref, o_ref, st_ref, mk_ref, tm_ref, *, seq):
    c = CHUNK
    rows = GDN_GROUP * c
    st_ref[...] = jnp.zeros_like(st_ref)
    @pl.when(pl.program_id(0) == 0)
    def _():
        causal, _, tril, _ = _chunk_masks(rows, c)
        mk_ref[...] = causal.astype(F32)
        tm_ref[...] = tril

    w2 = w2_ref[...]
    bg = bg_ref[...]
    nw = nw_ref[...]
    lane1 = lax.broadcasted_iota(I32, (1, LANES), 1)

    def group_body(gi, _):
        r0 = pl.multiple_of(gi * rows, rows)
        gs = g_ref[pl.ds(r0, rows), :]
        log_a = _log_sigmoid(_dot_hi(gs, w2) + bg) * (1.0 / GLA_TAU)
        cum = _dot_01(tm_ref[...], log_a)
        cum_last = _chunk_last_rows(cum, c)
        q = q_ref[pl.ds(r0, rows), :].astype(F32)
        k = k_ref[pl.ds(r0, rows), :].astype(F32)
        v = v_ref[pl.ds(r0, rows), :]
        r = r_ref[pl.ds(r0, rows), :].astype(F32)
        q_t = q * jnp.exp(cum) * (GLA_DK ** -0.5)
        k_t = k * jnp.exp(-cum)
        k_dec = k * jnp.exp(cum_last - cum)
        cdec = jnp.exp(cum_last)
        outs = []
        for h in range(GLA_HEADS):
            p, hh = divmod(h, 2)
            sl = slice(p * LANES, (p + 1) * LANES)
            head_lanes = (lane1 >= GLA_DK * hh) & (lane1 < GLA_DK * (hh + 1))
            qh = jnp.where(head_lanes, q_t[:, sl], 0.0).astype(BF16)
            kdh = jnp.where(head_lanes, k_dec[:, sl], 0.0).astype(BF16)
            attn = jnp.where(mk_ref[...] > 0.5, _dot_nt(qh, k_t[:, sl]), 0.0)
            vh = v[:, h * GLA_DV:(h + 1) * GLA_DV]
            intra = _dot(attn, vh)
            state_t = st_ref[h]
            inter = []
            for j in range(GDN_GROUP):
                js = slice(j * c, (j + 1) * c)
                inter.append(_dot_nt(qh[js], state_t))
                state_t = state_t * cdec[j * c:j * c + 1, sl] + _dot_tn(vh[js], kdh[js])
            st_ref[h] = state_t
            out = intra + jnp.concatenate(inter, axis=0)
            rms = out * lax.rsqrt(jnp.mean(out * out, axis=-1, keepdims=True) + RMS_EPS) * nw
            outs.append(rms * _silu(r[:, h * GLA_DV:(h + 1) * GLA_DV]))
        o_ref[pl.ds(r0, rows), :] = jnp.concatenate(outs, axis=1).astype(o_ref.dtype)
        return 0

    lax.fori_loop(0, seq // rows, group_body, 0)


def _gla(proj_big, gates, w2_pad, b_gate, norm_w, bsz, seq):
    n = bsz * seq
    return pl.pallas_call(
        functools.partial(_gla_kernel, seq=seq),
        grid=(bsz,),
        in_specs=[
            pl.BlockSpec((seq, GLA_QK), lambda b: (b, C_GLA_Q // GLA_QK)),
            pl.BlockSpec((seq, GLA_QK), lambda b: (b, C_GLA_K // GLA_QK)),
            pl.BlockSpec((seq, GLA_V), lambda b: (b, C_GLA_V // GLA_V)),
            pl.BlockSpec((seq, GLA_V), lambda b: (b, C_GLA_R // GLA_V)),
            pl.BlockSpec((seq, LANES), lambda b: (b, 0)),
            pl.BlockSpec((LANES, GLA_QK), lambda b: (0, 0)),
            pl.BlockSpec((1, GLA_QK), lambda b: (0, 0)),
            pl.BlockSpec((1, GLA_DV), lambda b: (0, 0)),
        ],
        out_specs=pl.BlockSpec((seq, GLA_V), lambda b: (b, 0)),
        out_shape=jax.ShapeDtypeStruct((n, GLA_V), BF16),
        scratch_shapes=[pltpu.VMEM((GLA_HEADS, GLA_DV, LANES), F32),
                        pltpu.VMEM((GDN_GROUP * CHUNK, GDN_GROUP * CHUNK), F32),
                        pltpu.VMEM((GDN_GROUP * CHUNK, GDN_GROUP * CHUNK), BF16)],
        compiler_params=_cparams("arbitrary"),
        name="gla",
    )(proj_big, proj_big, proj_big, proj_big, gates, w2_pad, b_gate, norm_w)


def _merge_kernel(oa_ref, ob_ref, oc_ref, m0_ref, m1_ref, m2_ref, x_ref, pa_ref, pb_ref, pc_ref, bm_ref,
                  wo_ref, g_ref, b_ref, o_ref):
    bm = bm_ref[...]
    merged = _sigmoid(m0_ref[...].astype(F32) + bm[0:1]) * jnp.dot(oa_ref[...], pa_ref[...],
                                                                   preferred_element_type=F32)
    merged += _sigmoid(m1_ref[...].astype(F32) + bm[1:2]) * jnp.dot(ob_ref[...], pb_ref[...],
                                                                    preferred_element_type=F32)
    merged += _sigmoid(m2_ref[...].astype(F32) + bm[2:3]) * jnp.dot(oc_ref[...], pc_ref[...],
                                                                    preferred_element_type=F32)
    mix = jnp.dot(merged.astype(BF16), wo_ref[...], preferred_element_type=F32)
    o_ref[...] = _layer_norm(DEEPNORM_ALPHA * x_ref[...] + mix, g_ref[...], b_ref[...])


def _merge(o_a, o_b, o_c, proj_big, x, p_a, p_b, p_c, b_merge, w_out, ln_g, ln_b):
    n = x.shape[0]
    tm = min(512, n)
    mb = C_MERGE // D_MODEL
    row = lambda i: (i, 0)
    full = lambda i: (0, 0)
    return pl.pallas_call(
        _merge_kernel,
        grid=(n // tm,),
        in_specs=[
            pl.BlockSpec((tm, GDN_V), row), pl.BlockSpec((tm, FOX_W), row), pl.BlockSpec((tm, GLA_V), row),
            pl.BlockSpec((tm, D_MODEL), lambda i: (i, mb)),
            pl.BlockSpec((tm, D_MODEL), lambda i: (i, mb + 1)),
            pl.BlockSpec((tm, D_MODEL), lambda i: (i, mb + 2)),
            pl.BlockSpec((tm, D_MODEL), row),
            pl.BlockSpec((GDN_V, D_MODEL), full), pl.BlockSpec((FOX_W, D_MODEL), full),
            pl.BlockSpec((GLA_V, D_MODEL), full),
            pl.BlockSpec((N_BRANCH, D_MODEL), full),
            pl.BlockSpec((D_MODEL, D_MODEL), full),
            pl.BlockSpec((1, D_MODEL), full), pl.BlockSpec((1, D_MODEL), full),
        ],
        out_specs=pl.BlockSpec((tm, D_MODEL), row),
        out_shape=jax.ShapeDtypeStruct((n, D_MODEL), F32),
        compiler_params=_cparams("parallel"),
        name="merge_out",
    )(o_a, o_b, o_c, proj_big, proj_big, proj_big, x, p_a, p_b, p_c, b_merge, w_out, ln_g, ln_b)


TOK_ROWS = D_MODEL // LANES


def _store_token_tiles(t_ref, y):
    m = y.shape[0]
    for s in range(TOK_ROWS):
        t_ref[pl.ds(s, m, stride=TOK_ROWS), :] = y[:, s * LANES:(s + 1) * LANES]


def _load_token_tiles(t_ref, m):
    return jnp.concatenate([t_ref[pl.ds(s, m, stride=TOK_ROWS), :] for s in range(TOK_ROWS)], axis=1)


def _xattn_kernel(x_ref, kv_ref, wq_ref, wo_ref, g_ref, b_ref, o_ref, t_ref):
    x = x_ref[...]
    q = jnp.dot(x.astype(BF16), wq_ref[...], preferred_element_type=F32).astype(BF16)
    kv = kv_ref[0]
    xa = jnp.zeros(x.shape, F32)
    for h in range(XA_HEADS):
        sl = slice(h * XA_DH, (h + 1) * XA_DH)
        s = _dot_nt(q[:, sl], kv[:, sl]) * (XA_DH ** -0.5)
        s = s - jnp.max(s, axis=-1, keepdims=True)
        e = jnp.exp(s)
        pr = e / jnp.sum(e, axis=-1, keepdims=True)
        o_h = _dot(pr, kv[:, D_MODEL + h * XA_DH:D_MODEL + (h + 1) * XA_DH])
        xa = xa + jnp.dot(o_h.astype(BF16), wo_ref[sl, :], preferred_element_type=F32)
    y = _layer_norm(DEEPNORM_ALPHA * x + xa, g_ref[...], b_ref[...])
    o_ref[...] = y
    _store_token_tiles(t_ref, y)


def _xattn(x, kv, wq, wo, ln_g, ln_b, bsz, seq):
    n = x.shape[0]
    tm = min(512, seq)
    nt = seq // tm
    mem = kv.shape[1]
    full = lambda b, i: (0, 0)
    return pl.pallas_call(
        _xattn_kernel,
        grid=(bsz, nt),
        in_specs=[
            pl.BlockSpec((tm, D_MODEL), lambda b, i: (b * nt + i, 0)),
            pl.BlockSpec((1, mem, 2 * D_MODEL), lambda b, i: (b, 0, 0)),
            pl.BlockSpec((D_MODEL, D_MODEL), full), pl.BlockSpec((D_MODEL, D_MODEL), full),
            pl.BlockSpec((1, D_MODEL), full), pl.BlockSpec((1, D_MODEL), full),
        ],
        out_specs=[pl.BlockSpec((tm, D_MODEL), lambda b, i: (b * nt + i, 0)),
                   pl.BlockSpec((tm * TOK_ROWS, LANES), lambda b, i: (b * nt + i, 0))],
        out_shape=[jax.ShapeDtypeStruct((n, D_MODEL), F32), jax.ShapeDtypeStruct((n * TOK_ROWS, LANES), F32)],
        compiler_params=_cparams("parallel", "parallel"),
        name="xattn",
    )(x, kv, wq, wo, ln_g, ln_b)


R_ROWS = 8 + MOE_EXPERTS


def _router_kernel(x_ref, w_ref, b_ref, o_ref):
    tm = x_ref.shape[0]
    logits = lax.dot_general(w_ref[...], x_ref[...], (((1,), (1,)), ((), ())), precision=HIGHEST,
                             preferred_element_type=F32) + b_ref[...]
    sub = lax.broadcasted_iota(I32, (8, tm), 0)
    grp = jnp.where(sub < MOE_GROUPS, logits[0:8], NEG_BIG)
    gmax = jnp.max(grp, axis=0, keepdims=True)
    p_top = 1.0 / jnp.sum(jnp.exp(grp - gmax), axis=0, keepdims=True)
    g_sel = jnp.min(jnp.where(grp == gmax, sub, 8), axis=0, keepdims=True)
    sel = logits[8:16]
    for g in range(1, MOE_GROUPS):
        sel = jnp.where(g_sel == g, logits[8 + 8 * g:16 + 8 * g], sel)
    e = jnp.exp(sel - jnp.max(sel, axis=0, keepdims=True))
    p_in = e / jnp.sum(e, axis=0, keepdims=True)
    v1 = jnp.max(p_in, axis=0, keepdims=True)
    i1 = jnp.min(jnp.where(p_in == v1, sub, 8), axis=0, keepdims=True)
    rest = jnp.where(sub == i1, -1.0, p_in)
    v2 = jnp.max(rest, axis=0, keepdims=True)
    i2 = jnp.min(jnp.where(rest == v2, sub, 8), axis=0, keepdims=True)
    tot = v1 + v2
    base = g_sel * MOE_PER_GROUP
    rows = [(base + i1).astype(F32), (base + i2).astype(F32), v1 / tot * p_top, v2 / tot * p_top]
    out = jnp.zeros((8, tm), F32)
    for r, val in enumerate(rows):
        out = jnp.where(sub == r, val, out)
    o_ref[...] = out


def _router(x, w_rt, b_rt):
    n = x.shape[0]
    tm = min(512, n)
    return pl.pallas_call(
        _router_kernel,
        grid=(n // tm,),
        in_specs=[pl.BlockSpec((tm, D_MODEL), lambda i: (i, 0)),
                  pl.BlockSpec((R_ROWS, D_MODEL), lambda i: (0, 0)),
                  pl.BlockSpec((R_ROWS, 1), lambda i: (0, 0))],
        out_specs=pl.BlockSpec((8, tm), lambda i: (0, i)),
        out_shape=jax.ShapeDtypeStruct((8, n), F32),
        compiler_params=_cparams("parallel"),
        name="router",
    )(x, w_rt, b_rt)


def _slots_kernel(r_ref, dest_ref, meta_ref, cnt_ref, run_ref, pst_ref, *, tb, nmeta):
    ph = pl.program_id(0)
    i = pl.program_id(1)
    ex = lax.broadcasted_iota(I32, (MOE_EXPERTS, tb), 0).astype(F32)
    oh0 = (ex == r_ref[0:1, :]).astype(F32)
    oh1 = (ex == r_ref[1:2, :]).astype(F32)

    @pl.when((ph == 0) & (i == 0))
    def _():
        cnt_ref[...] = jnp.zeros_like(cnt_ref)

    @pl.when(ph == 0)
    def _():
        cnt_ref[...] += jnp.sum(oh0 + oh1, axis=1, keepdims=True)

    @pl.when((ph == 1) & (i == 0))
    def _():
        cnt = cnt_ref[...]
        padded = jnp.floor((cnt + (MOE_BLOCK - 1)) * (1.0 / MOE_BLOCK)) * MOE_BLOCK
        strict = _tri(MOE_EXPERTS, "strict").astype(F32)
        pstart = _dot_hi(strict, padded)
        pst_ref[...] = pstart
        run_ref[...] = jnp.zeros_like(run_ref)
        pend = (pstart + padded)[:, 0:1]
        blk0 = lax.broadcasted_iota(I32, (MOE_EXPERTS, nmeta), 1).astype(F32) * MOE_BLOCK
        be = jnp.sum((pend <= blk0).astype(F32), axis=0, keepdims=True)
        be = jnp.minimum(be, MOE_EXPERTS - 1.0)
        meta_ref[...] = jnp.broadcast_to(be, (8, nmeta)).astype(I32)

    @pl.when(ph == 1)
    def _():
        triu = _tri(tb, "upper").astype(BF16)
        cum0 = jnp.dot(oh0.astype(BF16), triu, preferred_element_type=F32)
        cum1 = jnp.dot(oh1.astype(BF16), triu, preferred_element_type=F32)
        tot0 = cum0[:, tb - 1:tb]
        tot1 = cum1[:, tb - 1:tb]
        base = run_ref[:, 0:1] + pst_ref[:, 0:1]
        d0 = jnp.sum(oh0 * (cum0 - 1.0 + base), axis=0, keepdims=True)
        d1 = jnp.sum(oh1 * (cum1 - 1.0 + base + tot0), axis=0, keepdims=True)
        run_ref[...] += tot0 + tot1
        sub = lax.broadcasted_iota(I32, (8, tb), 0)
        dest_ref[...] = jnp.where(sub == 0, d0, jnp.where(sub == 1, d1, 0.0)).astype(I32)


def _slots(route, n_blocks):
    n = route.shape[1]
    tb = min(256, n)
    nmeta = -(-n_blocks // LANES) * LANES
    return pl.pallas_call(
        functools.partial(_slots_kernel, tb=tb, nmeta=nmeta),
        grid=(2, n // tb),
        in_specs=[pl.BlockSpec((8, tb), lambda ph, i: (0, i))],
        out_specs=[pl.BlockSpec((8, tb), lambda ph, i: (0, i * ph)), pl.BlockSpec((8, nmeta), lambda ph, i: (0, 0))],
        out_shape=[jax.ShapeDtypeStruct((8, n), I32), jax.ShapeDtypeStruct((8, nmeta), I32)],
        scratch_shapes=[pltpu.VMEM((MOE_EXPERTS, LANES), F32), pltpu.VMEM((MOE_EXPERTS, LANES), F32),
                        pltpu.VMEM((MOE_EXPERTS, LANES), F32)],
        compiler_params=_cparams("arbitrary", "arbitrary"),
        name="moe_slots",
    )(route)


DMA_UNROLL = 8


def _tile_copy(src_ref, src_tok, dst_ref, dst_tok, sem):
    src = src_ref.at[pl.ds(pl.multiple_of(src_tok * TOK_ROWS, TOK_ROWS), TOK_ROWS), :]
    dst = dst_ref.at[pl.ds(pl.multiple_of(dst_tok * TOK_ROWS, TOK_ROWS), TOK_ROWS), :]
    return pltpu.make_async_copy(src, dst, sem)


def _dispatch_kernel(dest_ref, xt_ref, xs_in_ref, xs_ref, sem, *, tb):
    del xs_in_ref

    def start(r, _):
        _tile_copy(xt_ref, r, xs_ref, dest_ref[0, r], sem).start(priority=0)
        _tile_copy(xt_ref, r, xs_ref, dest_ref[1, r], sem).start(priority=1)
        return 0

    def wait(r, _):
        _tile_copy(xt_ref, r, xs_ref, dest_ref[0, r], sem).wait()
        _tile_copy(xt_ref, r, xs_ref, dest_ref[1, r], sem).wait()
        return 0

    lax.fori_loop(0, tb, start, 0, unroll=DMA_UNROLL)
    lax.fori_loop(0, tb, wait, 0, unroll=DMA_UNROLL)


def _dispatch(dest, x_tiles, total):
    n = x_tiles.shape[0] // TOK_ROWS
    tb = min(256, n)
    xs0 = jnp.zeros((total * TOK_ROWS, LANES), F32)
    return pl.pallas_call(
        functools.partial(_dispatch_kernel, tb=tb),
        grid=(n // tb,),
        in_specs=[pl.BlockSpec((8, tb), lambda i: (0, i), memory_space=pltpu.SMEM),
                  pl.BlockSpec((tb * TOK_ROWS, LANES), lambda i: (i, 0)),
                  pl.BlockSpec(memory_space=pl.ANY)],
        out_specs=pl.BlockSpec(memory_space=pl.ANY),
        out_shape=jax.ShapeDtypeStruct((total * TOK_ROWS, LANES), F32),
        scratch_shapes=[pltpu.SemaphoreType.DMA],
        input_output_aliases={2: 0},
        compiler_params=_cparams("arbitrary"),
        name="moe_dispatch",
    )(dest, x_tiles, xs0)


def _expert_kernel(be_ref, x_ref, wg_ref, wu_ref, wd_ref, o_ref):
    del be_ref
    x = _load_token_tiles(x_ref, MOE_BLOCK).astype(BF16)
    gate = jnp.dot(x, wg_ref[0], preferred_element_type=F32)
    up = jnp.dot(x, wu_ref[0], preferred_element_type=F32)
    hidden = (_silu(gate) * up).astype(BF16)
    _store_token_tiles(o_ref, jnp.dot(hidden, wd_ref[0], preferred_element_type=F32))


def _experts(blk_expert, xs, w_gate, w_up, w_down):
    rows = MOE_BLOCK * TOK_ROWS
    nb = xs.shape[0] // rows
    grid_spec = pltpu.PrefetchScalarGridSpec(
        num_scalar_prefetch=1,
        grid=(nb,),
        in_specs=[
            pl.BlockSpec((rows, LANES), lambda j, be: (j, 0)),
            pl.BlockSpec((1, D_MODEL, MOE_FF), lambda j, be: (be[j], 0, 0)),
            pl.BlockSpec((1, D_MODEL, MOE_FF), lambda j, be: (be[j], 0, 0)),
            pl.BlockSpec((1, MOE_FF, D_MODEL), lambda j, be: (be[j], 0, 0)),
        ],
        out_specs=pl.BlockSpec((rows, LANES), lambda j, be: (j, 0)),
    )
    return pl.pallas_call(
        _expert_kernel,
        grid_spec=grid_spec,
        out_shape=jax.ShapeDtypeStruct(xs.shape, F32),
        compiler_params=_cparams("arbitrary"),
        name="moe_experts",
    )(blk_expert, xs, w_gate, w_up, w_down)


def _combine_kernel(dest_ref, r_ref, x_ref, ys_ref, g_ref, b_ref, o_ref, buf_ref, sem, *, tb):
    def start(r, _):
        _tile_copy(ys_ref, dest_ref[0, r], buf_ref.at[0], r, sem).start(priority=0)
        _tile_copy(ys_ref, dest_ref[1, r], buf_ref.at[1], r, sem).start(priority=1)
        return 0

    def wait(r, _):
        _tile_copy(ys_ref, dest_ref[0, r], buf_ref.at[0], r, sem).wait()
        _tile_copy(ys_ref, dest_ref[1, r], buf_ref.at[1], r, sem).wait()
        return 0

    lax.fori_loop(0, tb, start, 0, unroll=DMA_UNROLL)
    eye = (lax.broadcasted_iota(I32, (8, LANES), 0) == lax.broadcasted_iota(I32, (8, LANES), 1)).astype(F32)
    w_cols = _dot_tn_hi(r_ref[...], eye)
    lax.fori_loop(0, tb, wait, 0, unroll=DMA_UNROLL)
    y = w_cols[:, 2:3] * _load_token_tiles(buf_ref.at[0], tb) + w_cols[:, 3:4] * _load_token_tiles(buf_ref.at[1], tb)
    o_ref[...] = _layer_norm(DEEPNORM_ALPHA * x_ref[...] + y, g_ref[...], b_ref[...])


def _combine(dest, route, x, ys, ln_g, ln_b):
    n = x.shape[0]
    tb = min(256, n)
    return pl.pallas_call(
        functools.partial(_combine_kernel, tb=tb),
        grid=(n // tb,),
        in_specs=[pl.BlockSpec((8, tb), lambda i: (0, i), memory_space=pltpu.SMEM),
                  pl.BlockSpec((8, tb), lambda i: (0, i)),
                  pl.BlockSpec((tb, D_MODEL), lambda i: (i, 0)),
                  pl.BlockSpec(memory_space=pl.ANY),
                  pl.BlockSpec((1, D_MODEL), lambda i: (0, 0)), pl.BlockSpec((1, D_MODEL), lambda i: (0, 0))],
        out_specs=pl.BlockSpec((tb, D_MODEL), lambda i: (i, 0)),
        out_shape=jax.ShapeDtypeStruct((n, D_MODEL), F32),
        scratch_shapes=[pltpu.VMEM((2, tb * TOK_ROWS, LANES), F32), pltpu.SemaphoreType.DMA],
        compiler_params=_cparams("arbitrary"),
        name="moe_combine",
    )(dest, route, x, ys, ln_g, ln_b)


def _lane_vec(vals, offset, width=LANES):
    return jnp.zeros((1, width), F32).at[0, offset:offset + vals.shape[0]].set(vals.astype(F32))


def _pack_w_in(w_in):
    big = jnp.concatenate([w_in[:, 0:1536], w_in[:, 1544:2056], w_in[:, 2056:3592], w_in[:, 3600:5136],
                           w_in[:, 5152:8224]], axis=1).astype(BF16)
    small = jnp.zeros((D_MODEL, LANES), F32)
    small = small.at[:, G_B:G_B + 4].set(w_in[:, 1536:1540])
    small = small.at[:, G_A:G_A + 4].set(w_in[:, 1540:1544])
    small = small.at[:, G_F:G_F + 8].set(w_in[:, 3592:3600])
    small = small.at[:, G_LR:G_LR + 16].set(w_in[:, 5136:5152])
    return big, small


def _layer(x, mem_flat, bsz, seq, w):
    n = bsz * seq
    w_big, w_small = _pack_w_in(w["w_in"])
    proj_big = _matmul(x, w_big, BF16, 1024, 1024)
    gates = _matmul(x, w_small, F32, 1024, LANES, hi=True)

    cum_row = _fox_cum(gates, _lane_vec(w["fox_f_bias"], G_F), bsz, seq)
    o_b = _fox_attention(proj_big, cum_row, bsz, seq)
    o_a = _gdn(proj_big, gates, w["gdn_conv_w"], _lane_vec(w["gdn_a_log"], G_A), _lane_vec(w["gdn_dt_bias"], G_A),
               w["gdn_norm_w"].reshape(1, GDN_DV), bsz, seq)
    w2_pad = jnp.zeros((LANES, GLA_QK), F32).at[G_LR:G_LR + GLA_RANK].set(w["gla_w_gate2"])
    o_c = _gla(proj_big, gates, w2_pad, w["gla_b_gate"].reshape(1, GLA_QK), w["gla_norm_w"].reshape(1, GLA_DV),
               bsz, seq)
    x = _merge(o_a, o_b, o_c, proj_big, x, w["p_gdn"].astype(BF16), w["p_fox"].astype(BF16),
               w["p_gla"].astype(BF16), w["b_merge"].reshape(N_BRANCH, D_MODEL), w["w_out"].astype(BF16),
               w["ln1_g"].reshape(1, D_MODEL), w["ln1_b"].reshape(1, D_MODEL))

    mem_len = mem_flat.shape[0] // bsz
    kv = _matmul(mem_flat, w["xa_wkv"].astype(BF16), BF16, 512, 1024).reshape(bsz, mem_len, 2 * D_MODEL)
    x, x_tiles = _xattn(x, kv, w["xa_wq"].astype(BF16), w["xa_wo"].astype(BF16), w["ln2_g"].reshape(1, D_MODEL),
                        w["ln2_b"].reshape(1, D_MODEL), bsz, seq)

    w_rt = jnp.zeros((R_ROWS, D_MODEL), F32).at[0:MOE_GROUPS].set(w["moe_w_group"].T).at[8:].set(w["moe_w_expert"].T)
    b_rt = jnp.zeros((R_ROWS, 1), F32).at[0:MOE_GROUPS, 0].set(w["moe_b_group"]).at[8:, 0].set(w["moe_b_expert"])
    route = _router(x, w_rt, b_rt)
    nk = 2 * n
    total = -(-nk // MOE_BLOCK) * MOE_BLOCK + MOE_EXPERTS * MOE_BLOCK
    n_blocks = total // MOE_BLOCK
    dest, meta = _slots(route, n_blocks)
    xs = _dispatch(dest, x_tiles, total)
    ys = _experts(meta[0, :n_blocks], xs, w["moe_w_gate"].astype(BF16), w["moe_w_up"].astype(BF16),
                  w["moe_w_down"].astype(BF16))
    return _combine(dest, route, x, ys, w["ln3_g"].reshape(1, D_MODEL), w["ln3_b"].reshape(1, D_MODEL))


_PARAM_NAMES = ("w_in", "gdn_conv_w", "gdn_a_log", "gdn_dt_bias", "gdn_norm_w", "fox_f_bias", "gla_w_gate2",
                "gla_b_gate", "gla_norm_w", "p_gdn", "p_fox", "p_gla", "b_merge", "w_out", "ln1_g", "ln1_b",
                "xa_wq", "xa_wkv", "xa_wo", "ln2_g", "ln2_b", "moe_w_group", "moe_b_group", "moe_w_expert",
                "moe_b_expert", "moe_w_gate", "moe_w_up", "moe_w_down", "ln3_g", "ln3_b")


def kernel(x, mem, w_in, gdn_conv_w, gdn_a_log, gdn_dt_bias, gdn_norm_w, fox_f_bias, gla_w_gate2, gla_b_gate, gla_norm_w, p_gdn, p_fox, p_gla, b_merge, w_out, ln1_g, ln1_b, xa_wq, xa_wkv, xa_wo, ln2_g, ln2_b, moe_w_group, moe_b_group, moe_w_expert, moe_b_expert, moe_w_gate, moe_w_up, moe_w_down, ln3_g, ln3_b):
    params = (w_in, gdn_conv_w, gdn_a_log, gdn_dt_bias, gdn_norm_w, fox_f_bias, gla_w_gate2, gla_b_gate, gla_norm_w,
              p_gdn, p_fox, p_gla, b_merge, w_out, ln1_g, ln1_b, xa_wq, xa_wkv, xa_wo, ln2_g, ln2_b, moe_w_group,
              moe_b_group, moe_w_expert, moe_b_expert, moe_w_gate, moe_w_up, moe_w_down, ln3_g, ln3_b)
    bsz, seq, d = x.shape
    h = x.reshape(bsz * seq, d)
    mem_flat = mem.reshape(bsz * mem.shape[1], d)
    for l in range(w_in.shape[0]):
        h = _layer(h, mem_flat, bsz, seq, {name: p[l] for name, p in zip(_PARAM_NAMES, params)})
    return h.reshape(bsz, seq, d)
```

```python
import functools

import jax
import jax.numpy as jnp
from jax import lax
from jax.experimental import pallas as pl
from jax.experimental.pallas import tpu as pltpu

F32 = jnp.float32
BF16 = jnp.bfloat16
I32 = jnp.int32
HIGHEST = lax.Precision.HIGHEST

D_MODEL = 1024
DEPTH = 2
GDN_HEADS = 4
GDN_DK = 128
GDN_DV = 128
GDN_QK = GDN_HEADS * GDN_DK
GDN_V = GDN_HEADS * GDN_DV
GDN_CONV = 4
CHUNK = 64
FOX_HEADS = 8
FOX_DH = 64
FOX_W = FOX_HEADS * FOX_DH
GLA_HEADS = 4
GLA_DK = 64
GLA_DV = 128
GLA_QK = GLA_HEADS * GLA_DK
GLA_V = GLA_HEADS * GLA_DV
GLA_RANK = 16
GLA_TAU = 16.0
N_BRANCH = 3
XA_HEADS = 4
XA_DH = D_MODEL // XA_HEADS
MOE_GROUPS = 4
MOE_PER_GROUP = 8
MOE_EXPERTS = MOE_GROUPS * MOE_PER_GROUP
MOE_FF = D_MODEL // 4
MOE_BLOCK = 256
DEEPNORM_ALPHA = (2 * DEPTH) ** 0.25
LN_EPS = 1e-5
RMS_EPS = 1e-6

LANES = 128
CONV_PAD = 16
NEG_BIG = -1e30
VMEM_LIMIT = 56 * 1024 * 1024

C_GDN_QKV = 0
C_GDN_Z = 1536
C_FOX_Q = 2048
C_FOX_K = 2560
C_FOX_V = 3072
C_GLA_Q = 3584
C_GLA_K = 3840
C_GLA_V = 4096
C_GLA_R = 4608
C_MERGE = 5120
N_BIG = 8192
G_B, G_A, G_F, G_LR = 0, 4, 8, 16


def _cparams(*sem):
    return pltpu.CompilerParams(dimension_semantics=sem, vmem_limit_bytes=VMEM_LIMIT)


def _sigmoid(x):
    return 1.0 / (1.0 + jnp.exp(-x))


def _silu(x):
    return x * _sigmoid(x)


def _softplus(x):
    return jnp.maximum(x, 0.0) + jnp.log(1.0 + jnp.exp(-jnp.abs(x)))


def _log_sigmoid(x):
    return jnp.minimum(x, 0.0) - jnp.log(1.0 + jnp.exp(-jnp.abs(x)))


def _layer_norm(y, g, b):
    mu = jnp.mean(y, axis=-1, keepdims=True)
    yc = y - mu
    var = jnp.mean(yc * yc, axis=-1, keepdims=True)
    return yc * lax.rsqrt(var + LN_EPS) * g + b


def _dot(a, b):
    return jnp.dot(a.astype(BF16), b.astype(BF16), preferred_element_type=F32)


def _dot_nt(a, b):
    return lax.dot_general(a.astype(BF16), b.astype(BF16), (((1,), (1,)), ((), ())),
                           preferred_element_type=F32)


def _dot_tn(a, b):
    return lax.dot_general(a.astype(BF16), b.astype(BF16), (((0,), (0,)), ((), ())),
                           preferred_element_type=F32)


def _dot_hi(a, b):
    return jnp.dot(a, b, precision=HIGHEST, preferred_element_type=F32)


def _dot_tn_hi(a, b):
    return lax.dot_general(a, b, (((0,), (0,)), ((), ())), precision=HIGHEST, preferred_element_type=F32)


def _tri(n, kind):
    r = lax.broadcasted_iota(I32, (n, n), 0)
    c = lax.broadcasted_iota(I32, (n, n), 1)
    if kind == "lower":
        return r >= c
    if kind == "strict":
        return r > c
    return r <= c


def _chunk_masks(rows, c):
    r = lax.broadcasted_iota(I32, (rows, rows), 0)
    col = lax.broadcasted_iota(I32, (rows, rows), 1)
    same = (r | (c - 1)) == (col | (c - 1))
    causal = same & (r >= col)
    return causal, same & (r > col), causal.astype(BF16), (same & (r <= col)).astype(BF16)


def _split2(x):
    hi = x.astype(BF16)
    return hi, (x - hi.astype(F32)).astype(BF16)


def _dot_01(m01, x):
    hi, lo = _split2(x)
    return jnp.dot(m01, hi, preferred_element_type=F32) + jnp.dot(m01, lo, preferred_element_type=F32)


def _dot_tn_01(x, m01):
    hi, lo = _split2(x)
    dims = (((0,), (0,)), ((), ()))
    return (lax.dot_general(hi, m01, dims, preferred_element_type=F32)
            + lax.dot_general(lo, m01, dims, preferred_element_type=F32))


def _chunk_last_rows(x, c):
    rows = x.shape[0]
    r = lax.broadcasted_iota(I32, (rows, 1), 0)
    out = jnp.broadcast_to(x[c - 1:c, :], x.shape)
    for j in range(1, rows // c):
        out = jnp.where(r >= j * c, x[(j + 1) * c - 1:(j + 1) * c, :], out)
    return out


def _mm_kernel(a_ref, b_ref, o_ref):
    o_ref[...] = jnp.dot(a_ref[...].astype(BF16), b_ref[...], preferred_element_type=F32).astype(o_ref.dtype)


def _mm_hi_kernel(a_ref, b_ref, o_ref):
    o_ref[...] = _dot_hi(a_ref[...].astype(F32), b_ref[...]).astype(o_ref.dtype)


def _matmul(a, b, out_dtype, tm, tn, hi=False):
    m, k = a.shape
    n = b.shape[1]
    tm, tn = min(tm, m), min(tn, n)
    return pl.pallas_call(
        _mm_hi_kernel if hi else _mm_kernel,
        grid=(m // tm, n // tn),
        in_specs=[pl.BlockSpec((tm, k), lambda i, j: (i, 0)), pl.BlockSpec((k, tn), lambda i, j: (0, j))],
        out_specs=pl.BlockSpec((tm, tn), lambda i, j: (i, j)),
        out_shape=jax.ShapeDtypeStruct((m, n), out_dtype),
        compiler_params=_cparams("parallel", "parallel"),
        name="proj_hi" if hi else "proj",
    )(a, b)


def _fcum_kernel(g_ref, fb_ref, row_ref, *, seq, blk):
    triu = _tri(blk, "upper").astype(F32)
    carry = jnp.zeros((LANES, 1), F32)
    for i in range(seq // blk):
        lf = _log_sigmoid(g_ref[i * blk:(i + 1) * blk, :] + fb_ref[...])
        cum_t = _dot_tn_hi(lf, triu) + carry
        row_ref[0, :, i * blk:(i + 1) * blk] = cum_t[G_F:G_F + FOX_HEADS, :]
        carry = cum_t[:, blk - 1:blk]


def _fox_cum(gates, fb_vec, bsz, seq):
    blk = min(256, seq)
    return pl.pallas_call(
        functools.partial(_fcum_kernel, seq=seq, blk=blk),
        grid=(bsz,),
        in_specs=[pl.BlockSpec((seq, LANES), lambda b: (b, 0)), pl.BlockSpec((1, LANES), lambda b: (0, 0))],
        out_specs=pl.BlockSpec((1, FOX_HEADS, seq), lambda b: (b, 0, 0)),
        out_shape=jax.ShapeDtypeStruct((bsz, FOX_HEADS, seq), F32),
        compiler_params=_cparams("parallel"),
        name="fox_cum",
    )(gates, fb_vec)


FOX_GROUP = 8


def _fox_kernel(q_ref, k_ref, v_ref, cr_ref, o_ref, va_ref, *, tq, seq):
    grp = pl.program_id(1)
    qi = pl.program_id(2)
    lane = lax.broadcasted_iota(I32, (tq, LANES), 1)
    diag = lax.broadcasted_iota(I32, (tq, tq), 1) <= lax.broadcasted_iota(I32, (tq, tq), 0)
    scale = jnp.asarray(FOX_DH ** -0.5, BF16)

    def head_lanes(g, rows):
        ln = lax.broadcasted_iota(I32, (rows, LANES), 1)
        return (ln >= FOX_DH * (g % 2)) & (ln < FOX_DH * (g % 2 + 1))

    @pl.when(qi == 0)
    def _():
        for g in range(FOX_GROUP):
            vp = v_ref[:, (g // 2) * LANES:(g // 2 + 1) * LANES]
            va_ref[g] = jnp.where(head_lanes(g, seq), vp, jnp.ones_like(vp))

    qs = []
    for g in range(FOX_GROUP):
        qp = q_ref[:, (g // 2) * LANES:(g // 2 + 1) * LANES]
        qs.append(jnp.where(head_lanes(g, tq), qp, jnp.zeros_like(qp)) * scale)

    def tile(j, carry, masked):
        start = pl.multiple_of(j * tq, tq)
        out = []
        for g in range(FOX_GROUP):
            m, acc = carry[g]
            ps = slice((g // 2) * LANES, (g // 2 + 1) * LANES)
            c_k = cr_ref[0, pl.ds(FOX_GROUP * grp + g, 1), pl.ds(start, tq)]
            s = _dot_nt(qs[g], k_ref[pl.ds(start, tq), ps]) - c_k
            if masked:
                s = jnp.where(diag, s, NEG_BIG)
            m_new = jnp.maximum(m, jnp.max(s, axis=1, keepdims=True))
            pr = jnp.exp(s - m_new).astype(BF16)
            acc = jnp.exp(m - m_new) * acc + jnp.dot(pr, va_ref[g, pl.ds(start, tq), :], preferred_element_type=F32)
            out.append((m_new, acc))
        return tuple(out)

    init = tuple((jnp.full((tq, 1), NEG_BIG, F32), jnp.zeros((tq, LANES), F32)) for _ in range(FOX_GROUP))
    carry = lax.fori_loop(0, qi, lambda j, cy: tile(j, cy, False), init)
    carry = tile(qi, carry, True)
    res = [acc / pltpu.roll(acc, FOX_DH, axis=1) for _, acc in carry]
    for p in range(FOX_GROUP // 2):
        o_ref[:, p * LANES:(p + 1) * LANES] = jnp.where(lane < FOX_DH, res[2 * p], res[2 * p + 1]).astype(o_ref.dtype)


def _fox_attention(proj_big, cum_row, bsz, seq):
    tq = min(256, seq)
    nq = seq // tq
    n = bsz * seq
    gw = FOX_GROUP * FOX_DH
    qb, kb, vb = C_FOX_Q // gw, C_FOX_K // gw, C_FOX_V // gw
    return pl.pallas_call(
        functools.partial(_fox_kernel, tq=tq, seq=seq),
        grid=(bsz, FOX_HEADS // FOX_GROUP, nq),
        in_specs=[
            pl.BlockSpec((tq, gw), lambda b, p, i: (b * nq + i, qb + p)),
            pl.BlockSpec((seq, gw), lambda b, p, i: (b, kb + p)),
            pl.BlockSpec((seq, gw), lambda b, p, i: (b, vb + p)),
            pl.BlockSpec((1, FOX_HEADS, seq), lambda b, p, i: (b, 0, 0)),
        ],
        out_specs=pl.BlockSpec((tq, gw), lambda b, p, i: (b * nq + i, p)),
        out_shape=jax.ShapeDtypeStruct((n, FOX_W), BF16),
        scratch_shapes=[pltpu.VMEM((FOX_GROUP, seq, LANES), BF16)],
        compiler_params=_cparams("parallel", "parallel", "arbitrary"),
        name="fox_attn",
    )(proj_big, proj_big, proj_big, cum_row)


def _unit_lower_inverse(low, nil):
    n = low.shape[0]
    eye = (lax.broadcasted_iota(I32, (n, n), 0) == lax.broadcasted_iota(I32, (n, n), 1)).astype(F32)
    inv = eye - low
    power = _dot(low, low)
    span = 2
    while span < nil:
        inv = inv + _dot(inv, power)
        span *= 2
        if span < nil:
            power = _dot(power, power)
    return inv


GDN_GROUP = 4


def _gdn_kernel(qkv_ref, z_ref, g_ref, cw_ref, nega_ref, dt_ref, nw_ref, o_ref,
                pad_ref, u_ref, wq_ref, kd_ref, at_ref, cd_ref, st_ref, mk_ref, tm_ref, *, seq):
    c = CHUNK
    n_chunks = seq // c
    gr_rows = GDN_GROUP * c
    pad_ref[0:CONV_PAD, :] = jnp.zeros((CONV_PAD, 3 * GDN_QK), BF16)
    pad_ref[CONV_PAD:CONV_PAD + seq, :] = qkv_ref[...]
    st_ref[...] = jnp.zeros_like(st_ref)
    @pl.when(pl.program_id(0) == 0)
    def _():
        causal, strict, tril, triu = _chunk_masks(gr_rows, c)
        mk_ref[0] = causal.astype(F32)
        mk_ref[1] = strict.astype(F32)
        tm_ref[0] = tril
        tm_ref[1] = triu

    cw = cw_ref[...]
    neg_a = -jnp.exp(nega_ref[...])
    lane1 = lax.broadcasted_iota(I32, (1, LANES), 1)
    neg_a = jnp.where((lane1 >= G_A) & (lane1 < G_A + GDN_HEADS), neg_a, 0.0)
    dt = dt_ref[...]
    nw = nw_ref[...]
    sub8 = lax.broadcasted_iota(I32, (8, LANES), 0)

    def local_group(gi, _):
        r0 = pl.multiple_of(gi * gr_rows, gr_rows)
        win = pad_ref[pl.ds(r0, gr_rows + CONV_PAD), :]
        gs = g_ref[pl.ds(r0, gr_rows), :]
        beta_all = _sigmoid(gs)
        g_all = neg_a * _softplus(gs + dt)
        cum = _dot_01(tm_ref[0], g_all)
        cum_t = cum.T
        cum_last = _chunk_last_rows(cum, c)
        e_last = jnp.exp(cum_last)

        def conv(col0):
            cols = slice(col0, col0 + LANES)
            xw = win[:, cols].astype(F32)
            acc = xw[CONV_PAD - 3:CONV_PAD - 3 + gr_rows] * cw[0:1, cols]
            for i in range(1, GDN_CONV):
                acc = acc + xw[CONV_PAD - 3 + i:CONV_PAD - 3 + i + gr_rows] * cw[i:i + 1, cols]
            return _silu(acc)

        def l2n(x):
            return x * lax.rsqrt(jnp.sum(x * x, axis=-1, keepdims=True) + RMS_EPS)

        cd_tiles = [jnp.zeros((8, LANES), F32) for _ in range(GDN_GROUP)]
        for h in range(GDN_HEADS):
            hs = slice(h * GDN_DV, (h + 1) * GDN_DV)
            q = l2n(conv(h * GDN_DK)) * (GDN_DK ** -0.5)
            k = l2n(conv(GDN_QK + h * GDN_DK))
            v = conv(2 * GDN_QK + h * GDN_DV)
            beta = beta_all[:, G_B + h:G_B + h + 1]
            gc = cum[:, G_A + h:G_A + h + 1]
            gr = cum_t[G_A + h:G_A + h + 1, :]
            gl = cum_last[:, G_A + h:G_A + h + 1]
            eg = jnp.exp(gc)
            decay = jnp.exp(jnp.minimum(gc - gr, 0.0))
            kb = k * beta
            low = _dot_nt(kb, k) * (decay * mk_ref[1])
            t_inv = _unit_lower_inverse(low, c)
            uw = _dot(t_inv, jnp.concatenate([v * beta, kb * eg], axis=1))
            attn = (_dot_nt(q, k) * (decay * mk_ref[0])).astype(BF16)
            qd = (q * eg).astype(BF16)
            u_ref[pl.ds(r0, gr_rows), hs] = uw[:, 0:GDN_DV]
            kd_ref[pl.ds(r0, gr_rows), hs] = (k * jnp.exp(gl - gc)).astype(BF16)
            w = uw[:, GDN_DV:2 * GDN_DV].astype(BF16)
            for j in range(GDN_GROUP):
                js = slice(j * c, (j + 1) * c)
                wq_ref[pl.ds(2 * r0 + 2 * j * c, c), hs] = w[js]
                wq_ref[pl.ds(2 * r0 + 2 * j * c + c, c), hs] = qd[js]
                at_ref[h, pl.ds(r0 + j * c, c), :] = attn[js, js]
                cd_tiles[j] = jnp.where(sub8 == h, e_last[j * c:j * c + 1, G_A + h:G_A + h + 1], cd_tiles[j])
        for j in range(GDN_GROUP):
            cd_ref[pl.ds(pl.multiple_of(gi * (8 * GDN_GROUP), 8 * GDN_GROUP) + 8 * j, 8), :] = cd_tiles[j]
        return 0

    def recurrent_chunk(ci, _):
        r0 = pl.multiple_of(ci * c, c)
        z = z_ref[pl.ds(r0, c), :].astype(F32)
        cd_tile = cd_ref[pl.ds(pl.multiple_of(ci * 8, 8), 8), :]
        outs = []
        for h in range(GDN_HEADS):
            hs = slice(h * GDN_DV, (h + 1) * GDN_DV)
            state = st_ref[h]
            ws = jnp.dot(wq_ref[pl.ds(pl.multiple_of(2 * r0, 2 * c), 2 * c), hs], state.astype(BF16),
                         preferred_element_type=F32)
            v_new = (u_ref[pl.ds(r0, c), hs] - ws[0:c]).astype(BF16)
            out = ws[c:2 * c] + jnp.dot(at_ref[h, pl.ds(r0, c), :], v_new, preferred_element_type=F32)
            st_ref[h] = state * cd_tile[h:h + 1, :] + _dot_tn(kd_ref[pl.ds(r0, c), hs], v_new)
            rms = out * lax.rsqrt(jnp.mean(out * out, axis=-1, keepdims=True) + RMS_EPS) * nw
            outs.append(rms * _silu(z[:, hs]))
        o_ref[pl.ds(r0, c), :] = jnp.concatenate(outs, axis=1).astype(o_ref.dtype)
        return 0

    lax.fori_loop(0, n_chunks // GDN_GROUP, local_group, 0)
    lax.fori_loop(0, n_chunks, recurrent_chunk, 0)


def _gdn(proj_big, gates, conv_w, a_vec, dt_vec, norm_w, bsz, seq):
    n = bsz * seq
    return pl.pallas_call(
        functools.partial(_gdn_kernel, seq=seq),
        grid=(bsz,),
        in_specs=[
            pl.BlockSpec((seq, 3 * GDN_QK), lambda b: (b, 0)),
            pl.BlockSpec((seq, GDN_V), lambda b: (b, C_GDN_Z // GDN_V)),
            pl.BlockSpec((seq, LANES), lambda b: (b, 0)),
            pl.BlockSpec((GDN_CONV, 3 * GDN_QK), lambda b: (0, 0)),
            pl.BlockSpec((1, LANES), lambda b: (0, 0)),
            pl.BlockSpec((1, LANES), lambda b: (0, 0)),
            pl.BlockSpec((1, GDN_DV), lambda b: (0, 0)),
        ],
        out_specs=pl.BlockSpec((seq, GDN_V), lambda b: (b, 0)),
        out_shape=jax.ShapeDtypeStruct((n, GDN_V), BF16),
        scratch_shapes=[pltpu.VMEM((seq + CONV_PAD, 3 * GDN_QK), BF16),
                        pltpu.VMEM((seq, GDN_V), F32),
                        pltpu.VMEM((2 * seq, GDN_QK), BF16),
                        pltpu.VMEM((seq, GDN_QK), BF16),
                        pltpu.VMEM((GDN_HEADS, seq, CHUNK), BF16),
                        pltpu.VMEM((seq // CHUNK * 8, LANES), F32),
                        pltpu.VMEM((GDN_HEADS, GDN_DK, GDN_DV), F32),
                        pltpu.VMEM((2, GDN_GROUP * CHUNK, GDN_GROUP * CHUNK), F32),
                        pltpu.VMEM((2, GDN_GROUP * CHUNK, GDN_GROUP * CHUNK), BF16)],
        compiler_params=_cparams("arbitrary"),
        name="gdn",
    )(proj_big, proj_big, gates, conv_w, a_vec, dt_vec, norm_w)


def _gla_kernel(q_ref, k_ref, v_ref, r_ref, g_ref, w2_ref, bg_ref, nw_ref, o_ref, st_ref, mk_ref, tm_ref, *, seq):
    c = CHUNK
    rows = GDN_GROUP * c
    st_ref[...] = jnp.zeros_like(st_ref)
    @pl.when(pl.program_id(0) == 0)
    def _():
        causal, _, tril, _ = _chunk_masks(rows, c)
        mk_ref[...] = causal.astype(F32)
        tm_ref[...] = tril

    w2 = w2_ref[...]
    bg = bg_ref[...]
    nw = nw_ref[...]
    lane1 = lax.broadcasted_iota(I32, (1, LANES), 1)

    def group_body(gi, _):
        r0 = pl.multiple_of(gi * rows, rows)
        gs = g_ref[pl.ds(r0, rows), :]
        log_a = _log_sigmoid(_dot_hi(gs, w2) + bg) * (1.0 / GLA_TAU)
        cum = _dot_01(tm_ref[...], log_a)
        cum_last = _chunk_last_rows(cum, c)
        q = q_ref[pl.ds(r0, rows), :].astype(F32)
        k = k_ref[pl.ds(r0, rows), :].astype(F32)
        v = v_ref[pl.ds(r0, rows), :]
        r = r_ref[pl.ds(r0, rows), :].astype(F32)
        q_t = q * jnp.exp(cum) * (GLA_DK ** -0.5)
        k_t = k * jnp.exp(-cum)
        k_dec = k * jnp.exp(cum_last - cum)
        cdec = jnp.exp(cum_last)
        outs = []
        for h in range(GLA_HEADS):
            p, hh = divmod(h, 2)
            sl = slice(p * LANES, (p + 1) * LANES)
            head_lanes = (lane1 >= GLA_DK * hh) & (lane1 < GLA_DK * (hh + 1))
            qh = jnp.where(head_lanes, q_t[:, sl], 0.0).astype(BF16)
            kdh = jnp.where(head_lanes, k_dec[:, sl], 0.0).astype(BF16)
            attn = jnp.where(mk_ref[...] > 0.5, _dot_nt(qh, k_t[:, sl]), 0.0)
            vh = v[:, h * GLA_DV:(h + 1) * GLA_DV]
            intra = _dot(attn, vh)
            state_t = st_ref[h]
            inter = []
            for j in range(GDN_GROUP):
                js = slice(j * c, (j + 1) * c)
                inter.append(_dot_nt(qh[js], state_t))
                state_t = state_t * cdec[j * c:j * c + 1, sl] + _dot_tn(vh[js], kdh[js])
            st_ref[h] = state_t
            out = intra + jnp.concatenate(inter, axis=0)
            rms = out * lax.rsqrt(jnp.mean(out * out, axis=-1, keepdims=True) + RMS_EPS) * nw
            outs.append(rms * _silu(r[:, h * GLA_DV:(h + 1) * GLA_DV]))
        o_ref[pl.ds(r0, rows), :] = jnp.concatenate(outs, axis=1).astype(o_ref.dtype)
        return 0

    lax.fori_loop(0, seq // rows, group_body, 0)


def _gla(proj_big, gates, w2_pad, b_gate, norm_w, bsz, seq):
    n = bsz * seq
    return pl.pallas_call(
        functools.partial(_gla_kernel, seq=seq),
        grid=(bsz,),
        in_specs=[
            pl.BlockSpec((seq, GLA_QK), lambda b: (b, C_GLA_Q // GLA_QK)),
            pl.BlockSpec((seq, GLA_QK), lambda b: (b, C_GLA_K // GLA_QK)),
            pl.BlockSpec((seq, GLA_V), lambda b: (b, C_GLA_V // GLA_V)),
            pl.BlockSpec((seq, GLA_V), lambda b: (b, C_GLA_R // GLA_V)),
            pl.BlockSpec((seq, LANES), lambda b: (b, 0)),
            pl.BlockSpec((LANES, GLA_QK), lambda b: (0, 0)),
            pl.BlockSpec((1, GLA_QK), lambda b: (0, 0)),
            pl.BlockSpec((1, GLA_DV), lambda b: (0, 0)),
        ],
        out_specs=pl.BlockSpec((seq, GLA_V), lambda b: (b, 0)),
        out_shape=jax.ShapeDtypeStruct((n, GLA_V), BF16),
        scratch_shapes=[pltpu.VMEM((GLA_HEADS, GLA_DV, LANES), F32),
                        pltpu.VMEM((GDN_GROUP * CHUNK, GDN_GROUP * CHUNK), F32),
                        pltpu.VMEM((GDN_GROUP * CHUNK, GDN_GROUP * CHUNK), BF16)],
        compiler_params=_cparams("arbitrary"),
        name="gla",
    )(proj_big, proj_big, proj_big, proj_big, gates, w2_pad, b_gate, norm_w)


def _merge_kernel(oa_ref, ob_ref, oc_ref, m0_ref, m1_ref, m2_ref, x_ref, pa_ref, pb_ref, pc_ref, bm_ref,
                  wo_ref, g_ref, b_ref, o_ref):
    bm = bm_ref[...]
    merged = _sigmoid(m0_ref[...].astype(F32) + bm[0:1]) * jnp.dot(oa_ref[...], pa_ref[...],
                                                                   preferred_element_type=F32)
    merged += _sigmoid(m1_ref[...].astype(F32) + bm[1:2]) * jnp.dot(ob_ref[...], pb_ref[...],
                                                                    preferred_element_type=F32)
    merged += _sigmoid(m2_ref[...].astype(F32) + bm[2:3]) * jnp.dot(oc_ref[...], pc_ref[...],
                                                                    preferred_element_type=F32)
    mix = jnp.dot(merged.astype(BF16), wo_ref[...], preferred_element_type=F32)
    o_ref[...] = _layer_norm(DEEPNORM_ALPHA * x_ref[...] + mix, g_ref[...], b_ref[...])


def _merge(o_a, o_b, o_c, proj_big, x, p_a, p_b, p_c, b_merge, w_out, ln_g, ln_b):
    n = x.shape[0]
    tm = min(512, n)
    mb = C_MERGE // D_MODEL
    row = lambda i: (i, 0)
    full = lambda i: (0, 0)
    return pl.pallas_call(
        _merge_kernel,
        grid=(n // tm,),
        in_specs=[
            pl.BlockSpec((tm, GDN_V), row), pl.BlockSpec((tm, FOX_W), row), pl.BlockSpec((tm, GLA_V), row),
            pl.BlockSpec((tm, D_MODEL), lambda i: (i, mb)),
            pl.BlockSpec((tm, D_MODEL), lambda i: (i, mb + 1)),
            pl.BlockSpec((tm, D_MODEL), lambda i: (i, mb + 2)),
            pl.BlockSpec((tm, D_MODEL), row),
            pl.BlockSpec((GDN_V, D_MODEL), full), pl.BlockSpec((FOX_W, D_MODEL), full),
            pl.BlockSpec((GLA_V, D_MODEL), full),
            pl.BlockSpec((N_BRANCH, D_MODEL), full),
            pl.BlockSpec((D_MODEL, D_MODEL), full),
            pl.BlockSpec((1, D_MODEL), full), pl.BlockSpec((1, D_MODEL), full),
        ],
        out_specs=pl.BlockSpec((tm, D_MODEL), row),
        out_shape=jax.ShapeDtypeStruct((n, D_MODEL), F32),
        compiler_params=_cparams("parallel"),
        name="merge_out",
    )(o_a, o_b, o_c, proj_big, proj_big, proj_big, x, p_a, p_b, p_c, b_merge, w_out, ln_g, ln_b)


TOK_ROWS = D_MODEL // LANES


def _store_token_tiles(t_ref, y):
    m = y.shape[0]
    for s in range(TOK_ROWS):
        t_ref[pl.ds(s, m, stride=TOK_ROWS), :] = y[:, s * LANES:(s + 1) * LANES]


def _load_token_tiles(t_ref, m):
    return jnp.concatenate([t_ref[pl.ds(s, m, stride=TOK_ROWS), :] for s in range(TOK_ROWS)], axis=1)


def _xattn_kernel(x_ref, kv_ref, wq_ref, wo_ref, g_ref, b_ref, wr_ref, br_ref, o_ref, t_ref, rt_ref):
    x = x_ref[...]
    q = jnp.dot(x.astype(BF16), wq_ref[...], preferred_element_type=F32).astype(BF16)
    kv = kv_ref[0]
    xa = jnp.zeros(x.shape, F32)
    for h in range(XA_HEADS):
        sl = slice(h * XA_DH, (h + 1) * XA_DH)
        s = _dot_nt(q[:, sl], kv[:, sl]) * (XA_DH ** -0.5)
        s = s - jnp.max(s, axis=-1, keepdims=True)
        e = jnp.exp(s)
        pr = e / jnp.sum(e, axis=-1, keepdims=True)
        o_h = _dot(pr, kv[:, D_MODEL + h * XA_DH:D_MODEL + (h + 1) * XA_DH])
        xa = xa + jnp.dot(o_h.astype(BF16), wo_ref[sl, :], preferred_element_type=F32)
    y = _layer_norm(DEEPNORM_ALPHA * x + xa, g_ref[...], b_ref[...])
    o_ref[...] = y
    _store_token_tiles(t_ref, y)
    rt_ref[...] = _route_rows(y, wr_ref[...], br_ref[...])


def _xattn(x, kv, wq, wo, ln_g, ln_b, w_rt, b_rt, bsz, seq):
    n = x.shape[0]
    tm = min(512, seq)
    nt = seq // tm
    mem = kv.shape[1]
    full = lambda b, i: (0, 0)
    return pl.pallas_call(
        _xattn_kernel,
        grid=(bsz, nt),
        in_specs=[
            pl.BlockSpec((tm, D_MODEL), lambda b, i: (b * nt + i, 0)),
            pl.BlockSpec((1, mem, 2 * D_MODEL), lambda b, i: (b, 0, 0)),
            pl.BlockSpec((D_MODEL, D_MODEL), full), pl.BlockSpec((D_MODEL, D_MODEL), full),
            pl.BlockSpec((1, D_MODEL), full), pl.BlockSpec((1, D_MODEL), full),
            pl.BlockSpec((R_ROWS, D_MODEL), full), pl.BlockSpec((R_ROWS, 1), full),
        ],
        out_specs=[pl.BlockSpec((tm, D_MODEL), lambda b, i: (b * nt + i, 0)),
                   pl.BlockSpec((tm * TOK_ROWS, LANES), lambda b, i: (b * nt + i, 0)),
                   pl.BlockSpec((8, tm), lambda b, i: (0, b * nt + i))],
        out_shape=[jax.ShapeDtypeStruct((n, D_MODEL), F32), jax.ShapeDtypeStruct((n * TOK_ROWS, LANES), F32),
                   jax.ShapeDtypeStruct((8, n), F32)],
        compiler_params=_cparams("parallel", "parallel"),
        name="xattn",
    )(x, kv, wq, wo, ln_g, ln_b, w_rt, b_rt)


R_ROWS = 8 + MOE_EXPERTS


def _router_kernel(x_ref, w_ref, b_ref, o_ref):
    o_ref[...] = _route_rows(x_ref[...], w_ref[...], b_ref[...])


def _route_rows(x, w, b):
    tm = x.shape[0]
    logits = lax.dot_general(w, x, (((1,), (1,)), ((), ())), precision=HIGHEST,
                             preferred_element_type=F32) + b
    sub = lax.broadcasted_iota(I32, (8, tm), 0)
    grp = jnp.where(sub < MOE_GROUPS, logits[0:8], NEG_BIG)
    gmax = jnp.max(grp, axis=0, keepdims=True)
    p_top = 1.0 / jnp.sum(jnp.exp(grp - gmax), axis=0, keepdims=True)
    g_sel = jnp.min(jnp.where(grp == gmax, sub, 8), axis=0, keepdims=True)
    sel = logits[8:16]
    for g in range(1, MOE_GROUPS):
        sel = jnp.where(g_sel == g, logits[8 + 8 * g:16 + 8 * g], sel)
    e = jnp.exp(sel - jnp.max(sel, axis=0, keepdims=True))
    p_in = e / jnp.sum(e, axis=0, keepdims=True)
    v1 = jnp.max(p_in, axis=0, keepdims=True)
    i1 = jnp.min(jnp.where(p_in == v1, sub, 8), axis=0, keepdims=True)
    rest = jnp.where(sub == i1, -1.0, p_in)
    v2 = jnp.max(rest, axis=0, keepdims=True)
    i2 = jnp.min(jnp.where(rest == v2, sub, 8), axis=0, keepdims=True)
    tot = v1 + v2
    base = g_sel * MOE_PER_GROUP
    rows = [(base + i1).astype(F32), (base + i2).astype(F32), v1 / tot * p_top, v2 / tot * p_top]
    out = jnp.zeros((8, tm), F32)
    for r, val in enumerate(rows):
        out = jnp.where(sub == r, val, out)
    return out


def _router(x, w_rt, b_rt):
    n = x.shape[0]
    tm = min(512, n)
    return pl.pallas_call(
        _router_kernel,
        grid=(n // tm,),
        in_specs=[pl.BlockSpec((tm, D_MODEL), lambda i: (i, 0)),
                  pl.BlockSpec((R_ROWS, D_MODEL), lambda i: (0, 0)),
                  pl.BlockSpec((R_ROWS, 1), lambda i: (0, 0))],
        out_specs=pl.BlockSpec((8, tm), lambda i: (0, i)),
        out_shape=jax.ShapeDtypeStruct((8, n), F32),
        compiler_params=_cparams("parallel"),
        name="router",
    )(x, w_rt, b_rt)


def _slots_kernel(r_ref, dest_ref, meta_ref, cnt_ref, run_ref, pst_ref, *, tb, nmeta):
    ph = pl.program_id(0)
    i = pl.program_id(1)
    ex = lax.broadcasted_iota(I32, (MOE_EXPERTS, tb), 0).astype(F32)
    oh0 = (ex == r_ref[0:1, :]).astype(F32)
    oh1 = (ex == r_ref[1:2, :]).astype(F32)

    @pl.when((ph == 0) & (i == 0))
    def _():
        cnt_ref[...] = jnp.zeros_like(cnt_ref)

    @pl.when(ph == 0)
    def _():
        cnt_ref[...] += jnp.sum(oh0 + oh1, axis=1, keepdims=True)

    @pl.when((ph == 1) & (i == 0))
    def _():
        cnt = cnt_ref[...]
        padded = jnp.floor((cnt + (MOE_BLOCK - 1)) * (1.0 / MOE_BLOCK)) * MOE_BLOCK
        strict = _tri(MOE_EXPERTS, "strict").astype(F32)
        pstart = _dot_hi(strict, padded)
        pst_ref[...] = pstart
        run_ref[...] = jnp.zeros_like(run_ref)
        pend = (pstart + padded)[:, 0:1]
        blk0 = lax.broadcasted_iota(I32, (MOE_EXPERTS, nmeta), 1).astype(F32) * MOE_BLOCK
        be = jnp.sum((pend <= blk0).astype(F32), axis=0, keepdims=True)
        be = jnp.minimum(be, MOE_EXPERTS - 1.0)
        meta_ref[...] = jnp.broadcast_to(be, (8, nmeta)).astype(I32)

    @pl.when(ph == 1)
    def _():
        triu = _tri(tb, "upper").astype(BF16)
        cum0 = jnp.dot(oh0.astype(BF16), triu, preferred_element_type=F32)
        cum1 = jnp.dot(oh1.astype(BF16), triu, preferred_element_type=F32)
        tot0 = cum0[:, tb - 1:tb]
        tot1 = cum1[:, tb - 1:tb]
        base = run_ref[:, 0:1] + pst_ref[:, 0:1]
        d0 = jnp.sum(oh0 * (cum0 - 1.0 + base), axis=0, keepdims=True)
        d1 = jnp.sum(oh1 * (cum1 - 1.0 + base + tot0), axis=0, keepdims=True)
        run_ref[...] += tot0 + tot1
        sub = lax.broadcasted_iota(I32, (8, tb), 0)
        dest_ref[...] = jnp.where(sub == 0, d0, jnp.where(sub == 1, d1, 0.0)).astype(I32)


def _slots(route, n_blocks):
    n = route.shape[1]
    tb = min(256, n)
    nmeta = -(-n_blocks // LANES) * LANES
    return pl.pallas_call(
        functools.partial(_slots_kernel, tb=tb, nmeta=nmeta),
        grid=(2, n // tb),
        in_specs=[pl.BlockSpec((8, tb), lambda ph, i: (0, i))],
        out_specs=[pl.BlockSpec((8, tb), lambda ph, i: (0, i * ph)), pl.BlockSpec((8, nmeta), lambda ph, i: (0, 0))],
        out_shape=[jax.ShapeDtypeStruct((8, n), I32), jax.ShapeDtypeStruct((8, nmeta), I32)],
        scratch_shapes=[pltpu.VMEM((MOE_EXPERTS, LANES), F32), pltpu.VMEM((MOE_EXPERTS, LANES), F32),
                        pltpu.VMEM((MOE_EXPERTS, LANES), F32)],
        compiler_params=_cparams("arbitrary", "arbitrary"),
        name="moe_slots",
    )(route)


DMA_UNROLL = 8


def _tile_copy(src_ref, src_tok, dst_ref, dst_tok, sem):
    src = src_ref.at[pl.ds(pl.multiple_of(src_tok * TOK_ROWS, TOK_ROWS), TOK_ROWS), :]
    dst = dst_ref.at[pl.ds(pl.multiple_of(dst_tok * TOK_ROWS, TOK_ROWS), TOK_ROWS), :]
    return pltpu.make_async_copy(src, dst, sem)


def _dispatch_kernel(dest_ref, xt_ref, xs_in_ref, xs_ref, sem, *, tb):
    del xs_in_ref

    def start(r, _):
        _tile_copy(xt_ref, r, xs_ref, dest_ref[0, r], sem).start(priority=0)
        _tile_copy(xt_ref, r, xs_ref, dest_ref[1, r], sem).start(priority=1)
        return 0

    def wait(r, _):
        _tile_copy(xt_ref, r, xs_ref, dest_ref[0, r], sem).wait()
        _tile_copy(xt_ref, r, xs_ref, dest_ref[1, r], sem).wait()
        return 0

    lax.fori_loop(0, tb, start, 0, unroll=DMA_UNROLL)
    lax.fori_loop(0, tb, wait, 0, unroll=DMA_UNROLL)


def _dispatch(dest, x_tiles, total):
    n = x_tiles.shape[0] // TOK_ROWS
    tb = min(256, n)
    xs0 = jnp.zeros((total * TOK_ROWS, LANES), F32)
    return pl.pallas_call(
        functools.partial(_dispatch_kernel, tb=tb),
        grid=(n // tb,),
        in_specs=[pl.BlockSpec((8, tb), lambda i: (0, i), memory_space=pltpu.SMEM),
                  pl.BlockSpec((tb * TOK_ROWS, LANES), lambda i: (i, 0)),
                  pl.BlockSpec(memory_space=pl.ANY)],
        out_specs=pl.BlockSpec(memory_space=pl.ANY),
        out_shape=jax.ShapeDtypeStruct((total * TOK_ROWS, LANES), F32),
        scratch_shapes=[pltpu.SemaphoreType.DMA],
        input_output_aliases={2: 0},
        compiler_params=_cparams("arbitrary"),
        name="moe_dispatch",
    )(dest, x_tiles, xs0)


def _expert_kernel(be_ref, x_ref, wg_ref, wu_ref, wd_ref, o_ref):
    del be_ref
    x = _load_token_tiles(x_ref, MOE_BLOCK).astype(BF16)
    gate = jnp.dot(x, wg_ref[0], preferred_element_type=F32)
    up = jnp.dot(x, wu_ref[0], preferred_element_type=F32)
    hidden = (_silu(gate) * up).astype(BF16)
    _store_token_tiles(o_ref, jnp.dot(hidden, wd_ref[0], preferred_element_type=F32))


def _experts(blk_expert, xs, w_gate, w_up, w_down):
    rows = MOE_BLOCK * TOK_ROWS
    nb = xs.shape[0] // rows
    grid_spec = pltpu.PrefetchScalarGridSpec(
        num_scalar_prefetch=1,
        grid=(nb,),
        in_specs=[
            pl.BlockSpec((rows, LANES), lambda j, be: (j, 0)),
            pl.BlockSpec((1, D_MODEL, MOE_FF), lambda j, be: (be[j], 0, 0)),
            pl.BlockSpec((1, D_MODEL, MOE_FF), lambda j, be: (be[j], 0, 0)),
            pl.BlockSpec((1, MOE_FF, D_MODEL), lambda j, be: (be[j], 0, 0)),
        ],
        out_specs=pl.BlockSpec((rows, LANES), lambda j, be: (j, 0)),
    )
    return pl.pallas_call(
        _expert_kernel,
        grid_spec=grid_spec,
        out_shape=jax.ShapeDtypeStruct(xs.shape, F32),
        compiler_params=_cparams("arbitrary"),
        name="moe_experts",
    )(blk_expert, xs, w_gate, w_up, w_down)


def _combine_kernel(dest_ref, r_ref, x_ref, ys_ref, g_ref, b_ref, o_ref, buf_ref, sem, *, tb):
    def start(r, _):
        _tile_copy(ys_ref, dest_ref[0, r], buf_ref.at[0], r, sem).start(priority=0)
        _tile_copy(ys_ref, dest_ref[1, r], buf_ref.at[1], r, sem).start(priority=1)
        return 0

    def wait(r, _):
        _tile_copy(ys_ref, dest_ref[0, r], buf_ref.at[0], r, sem).wait()
        _tile_copy(ys_ref, dest_ref[1, r], buf_ref.at[1], r, sem).wait()
        return 0

    lax.fori_loop(0, tb, start, 0, unroll=DMA_UNROLL)
    eye = (lax.broadcasted_iota(I32, (8, LANES), 0) == lax.broadcasted_iota(I32, (8, LANES), 1)).astype(F32)
    w_cols = _dot_tn_hi(r_ref[...], eye)
    lax.fori_loop(0, tb, wait, 0, unroll=DMA_UNROLL)
    y = w_cols[:, 2:3] * _load_token_tiles(buf_ref.at[0], tb) + w_cols[:, 3:4] * _load_token_tiles(buf_ref.at[1], tb)
    o_ref[...] = _layer_norm(DEEPNORM_ALPHA * x_ref[...] + y, g_ref[...], b_ref[...])


def _combine(dest, route, x, ys, ln_g, ln_b):
    n = x.shape[0]
    tb = min(256, n)
    return pl.pallas_call(
        functools.partial(_combine_kernel, tb=tb),
        grid=(n // tb,),
        in_specs=[pl.BlockSpec((8, tb), lambda i: (0, i), memory_space=pltpu.SMEM),
                  pl.BlockSpec((8, tb), lambda i: (0, i)),
                  pl.BlockSpec((tb, D_MODEL), lambda i: (i, 0)),
                  pl.BlockSpec(memory_space=pl.ANY),
                  pl.BlockSpec((1, D_MODEL), lambda i: (0, 0)), pl.BlockSpec((1, D_MODEL), lambda i: (0, 0))],
        out_specs=pl.BlockSpec((tb, D_MODEL), lambda i: (i, 0)),
        out_shape=jax.ShapeDtypeStruct((n, D_MODEL), F32),
        scratch_shapes=[pltpu.VMEM((2, tb * TOK_ROWS, LANES), F32), pltpu.SemaphoreType.DMA],
        compiler_params=_cparams("arbitrary"),
        name="moe_combine",
    )(dest, route, x, ys, ln_g, ln_b)


def _lane_vec(vals, offset, width=LANES):
    return jnp.zeros((1, width), F32).at[0, offset:offset + vals.shape[0]].set(vals.astype(F32))


def _pack_w_in(w_in):
    big = jnp.concatenate([w_in[:, 0:1536], w_in[:, 1544:2056], w_in[:, 2056:3592], w_in[:, 3600:5136],
                           w_in[:, 5152:8224]], axis=1).astype(BF16)
    small = jnp.zeros((D_MODEL, LANES), F32)
    small = small.at[:, G_B:G_B + 4].set(w_in[:, 1536:1540])
    small = small.at[:, G_A:G_A + 4].set(w_in[:, 1540:1544])
    small = small.at[:, G_F:G_F + 8].set(w_in[:, 3592:3600])
    small = small.at[:, G_LR:G_LR + 16].set(w_in[:, 5136:5152])
    return big, small


def _layer(x, mem_flat, bsz, seq, w):
    n = bsz * seq
    w_big, w_small = _pack_w_in(w["w_in"])
    proj_big = _matmul(x, w_big, BF16, 1024, 1024)
    gates = _matmul(x, w_small, F32, 1024, LANES, hi=True)

    cum_row = _fox_cum(gates, _lane_vec(w["fox_f_bias"], G_F), bsz, seq)
    o_b = _fox_attention(proj_big, cum_row, bsz, seq)
    o_a = _gdn(proj_big, gates, w["gdn_conv_w"], _lane_vec(w["gdn_a_log"], G_A), _lane_vec(w["gdn_dt_bias"], G_A),
               w["gdn_norm_w"].reshape(1, GDN_DV), bsz, seq)
    w2_pad = jnp.zeros((LANES, GLA_QK), F32).at[G_LR:G_LR + GLA_RANK].set(w["gla_w_gate2"])
    o_c = _gla(proj_big, gates, w2_pad, w["gla_b_gate"].reshape(1, GLA_QK), w["gla_norm_w"].reshape(1, GLA_DV),
               bsz, seq)
    x = _merge(o_a, o_b, o_c, proj_big, x, w["p_gdn"].astype(BF16), w["p_fox"].astype(BF16),
               w["p_gla"].astype(BF16), w["b_merge"].reshape(N_BRANCH, D_MODEL), w["w_out"].astype(BF16),
               w["ln1_g"].reshape(1, D_MODEL), w["ln1_b"].reshape(1, D_MODEL))

    mem_len = mem_flat.shape[0] // bsz
    kv = _matmul(mem_flat, w["xa_wkv"].astype(BF16), BF16, 512, 1024).reshape(bsz, mem_len, 2 * D_MODEL)
    w_rt = jnp.zeros((R_ROWS, D_MODEL), F32).at[0:MOE_GROUPS].set(w["moe_w_group"].T).at[8:].set(w["moe_w_expert"].T)
    b_rt = jnp.zeros((R_ROWS, 1), F32).at[0:MOE_GROUPS, 0].set(w["moe_b_group"]).at[8:, 0].set(w["moe_b_expert"])
    x, x_tiles, route = _xattn(x, kv, w["xa_wq"].astype(BF16), w["xa_wo"].astype(BF16),
                               w["ln2_g"].reshape(1, D_MODEL), w["ln2_b"].reshape(1, D_MODEL), w_rt, b_rt, bsz, seq)
    nk = 2 * n
    total = -(-nk // MOE_BLOCK) * MOE_BLOCK + MOE_EXPERTS * MOE_BLOCK
    n_blocks = total // MOE_BLOCK
    dest, meta = _slots(route, n_blocks)
    xs = _dispatch(dest, x_tiles, total)
    ys = _experts(meta[0, :n_blocks], xs, w["moe_w_gate"].astype(BF16), w["moe_w_up"].astype(BF16),
                  w["moe_w_down"].astype(BF16))
    return _combine(dest, route, x, ys, w["ln3_g"].reshape(1, D_MODEL), w["ln3_b"].reshape(1, D_MODEL))


_PARAM_NAMES = ("w_in", "gdn_conv_w", "gdn_a_log", "gdn_dt_bias", "gdn_norm_w", "fox_f_bias", "gla_w_gate2",
                "gla_b_gate", "gla_norm_w", "p_gdn", "p_fox", "p_gla", "b_merge", "w_out", "ln1_g", "ln1_b",
                "xa_wq", "xa_wkv", "xa_wo", "ln2_g", "ln2_b", "moe_w_group", "moe_b_group", "moe_w_expert",
                "moe_b_expert", "moe_w_gate", "moe_w_up", "moe_w_down", "ln3_g", "ln3_b")


def kernel(x, mem, w_in, gdn_conv_w, gdn_a_log, gdn_dt_bias, gdn_norm_w, fox_f_bias, gla_w_gate2, gla_b_gate, gla_norm_w, p_gdn, p_fox, p_gla, b_merge, w_out, ln1_g, ln1_b, xa_wq, xa_wkv, xa_wo, ln2_g, ln2_b, moe_w_group, moe_b_group, moe_w_expert, moe_b_expert, moe_w_gate, moe_w_up, moe_w_down, ln3_g, ln3_b):
    params = (w_in, gdn_conv_w, gdn_a_log, gdn_dt_bias, gdn_norm_w, fox_f_bias, gla_w_gate2, gla_b_gate, gla_norm_w,
              p_gdn, p_fox, p_gla, b_merge, w_out, ln1_g, ln1_b, xa_wq, xa_wkv, xa_wo, ln2_g, ln2_b, moe_w_group,
              moe_b_group, moe_w_expert, moe_b_expert, moe_w_gate, moe_w_up, moe_w_down, ln3_g, ln3_b)
    bsz, seq, d = x.shape
    h = x.reshape(bsz * seq, d)
    mem_flat = mem.reshape(bsz * mem.shape[1], d)
    for l in range(w_in.shape[0]):
        h = _layer(h, mem_flat, bsz, seq, {name: p[l] for name, p in zip(_PARAM_NAMES, params)})
    return h.reshape(bsz, seq, d)
```
